```python
import jax, jax.numpy as jnp
from jax import lax
import numpy as np

D_MODEL = 2048
BATCH = 4
SEQ = 4096
DEPTH = 4

GRID_W = 64
CTX_LEN = 256
D_MIX = D_MODEL
HG_WIDTH = D_MIX // 2
HG_HEADS = 8
HG_DK = HG_WIDTH // HG_HEADS
HG_DV = HG_WIDTH // HG_HEADS
HG_CHUNK = 64
RG_WIDTH = D_MIX - HG_WIDTH
RG_BLOCKS = 8
RG_BW = RG_WIDTH // RG_BLOCKS
RG_CONV = 4
RG_C = 8.0
FFN_CONV = 3
D_FF = 11 * D_MODEL // 4
N_PROJ = 5 * HG_WIDTH + 2 * RG_WIDTH
EPS = 1e-6

kernel_name = "hybrid_hgrn2_rglru_dit_trunk"


def rmsnorm(x, g):
    xf = x.astype(jnp.float32)
    y = xf * lax.rsqrt(jnp.mean(xf * xf, axis=-1, keepdims=True) + EPS)
    return (y * g.astype(jnp.float32)).astype(x.dtype)


def modulate(h, shift, scale):
    return h * (1.0 + scale) + shift


def dwconv(x, w, b, pad_left, pad_right):
    y = lax.conv_general_dilated(x, w[:, None, :].astype(x.dtype), window_strides=(1,),
                                 padding=[(pad_left, pad_right)],
                                 dimension_numbers=('NWC', 'WIO', 'NWC'),
                                 feature_group_count=x.shape[-1])
    return y + b


def grid_to_colmajor(t, rows):
    B, L = t.shape[:2]
    rest = t.shape[2:]
    return t.reshape(B, rows, GRID_W, *rest).swapaxes(1, 2).reshape(B, L, *rest)


def colmajor_to_grid(t, rows):
    B, L = t.shape[:2]
    rest = t.shape[2:]
    return t.reshape(B, GRID_W, rows, *rest).swapaxes(1, 2).reshape(B, L, *rest)


def split_proj(p):
    points = [HG_WIDTH, 2 * HG_WIDTH, 3 * HG_WIDTH, 4 * HG_WIDTH, 5 * HG_WIDTH,
              5 * HG_WIDTH + RG_WIDTH]
    return jnp.split(p, points, axis=-1)


def hgrn2_scan(q, k, v, log_f, s0):
    B, L, H, _ = q.shape
    DV = v.shape[-1]
    n = L // HG_CHUNK

    def chunks(t):
        return t.astype(jnp.float32).reshape(B, n, HG_CHUNK, H, t.shape[-1]).transpose(1, 0, 3, 2, 4)

    mask = jnp.tril(jnp.ones((HG_CHUNK, HG_CHUNK), dtype=bool))[:, :, None]

    def step(S, inp):
        qc, kc, vc, gc = inp
        b = jnp.cumsum(gc, axis=2)
        diff = b[:, :, :, None, :] - b[:, :, None, :, :]
        decay = jnp.where(mask, jnp.exp(jnp.where(mask, diff, 0.0)), 0.0)
        scores = jnp.einsum('bhtd,bhsd,bhtsd->bhts', qc, kc, decay)
        o = (jnp.einsum('bhts,bhse->bhte', scores, vc)
             + jnp.einsum('bhtd,bhde->bhte', qc * jnp.exp(b), S))
        b_last = b[:, :, -1:, :]
        S_new = (jnp.exp(b_last[:, :, 0, :])[..., None] * S
                 + jnp.einsum('bhsd,bhse->bhde', kc * jnp.exp(b_last - b), vc))
        return S_new, o

    S, o = lax.scan(step, s0, (chunks(q), chunks(k), chunks(v), chunks(log_f)))
    o = o.transpose(1, 0, 3, 2, 4).reshape(B, L, H, DV)
    return o.astype(v.dtype), S


def hgrn2_dir(q, z, v, lb, s0):
    B, L, _ = q.shape
    zf = z.astype(jnp.float32)
    log_f = jnp.log(lb + (1.0 - lb) * jax.nn.sigmoid(zf))
    k = (1.0 - lb) * jax.nn.sigmoid(-zf)
    heads = lambda t, d: t.reshape(B, L, HG_HEADS, d)
    return hgrn2_scan(heads(q, HG_DK), heads(k, HG_DK), heads(v, HG_DV), heads(log_f, HG_DK), s0)


def hgrn2_bidirectional(q, zf, zb, v, lb, s0_f, s0_b):
    o_f, s_f = hgrn2_dir(q, zf, v, lb[0], s0_f)
    o_b, s_b = hgrn2_dir(q[:, ::-1], zb[:, ::-1], v[:, ::-1], lb[1], s0_b)
    return o_f + o_b[:, ::-1], s_f, s_b


def hgrn2_readout(o, g, hg_g):
    B, L = o.shape[:2]
    o = rmsnorm(o, hg_g.reshape(HG_HEADS, HG_DV)).reshape(B, L, HG_WIDTH)
    return o * jax.nn.silu(g)


def linear_scan(a, b, h0):
    b = b.at[:, 0].add(a[:, 0] * h0)

    def combine(e1, e2):
        a1, b1 = e1
        a2, b2 = e2
        return a1 * a2, a2 * b1 + b2

    _, h = lax.associative_scan(combine, (a, b), axis=1)
    return h


def rglru_dir(xc, wa, ba, wx, bx, lam, h0):
    B, L, C = xc.shape
    xb = xc.reshape(B, L, RG_BLOCKS, RG_BW)
    r = jax.nn.sigmoid(jnp.einsum('blnc,ncd->blnd', xb, wa).reshape(B, L, C) + ba)
    i = jax.nn.sigmoid(jnp.einsum('blnc,ncd->blnd', xb, wx).reshape(B, L, C) + bx)
    log_a = -RG_C * r.astype(jnp.float32) * jax.nn.softplus(-lam.astype(jnp.float32))
    a = jnp.exp(log_a)
    mult = jnp.sqrt(-jnp.expm1(2.0 * log_a))
    bterm = mult * i.astype(jnp.float32) * xc.astype(jnp.float32)
    h = linear_scan(a, bterm, h0)
    return h.astype(xc.dtype), h[:, -1]


def rglru_bidirectional(xc, rg, h0_f, h0_b):
    wa, ba, wx, bx, lam = rg
    h_f, s_f = rglru_dir(xc, wa[0], ba[0], wx[0], bx[0], lam[0], h0_f)
    h_b, s_b = rglru_dir(xc[:, ::-1], wa[1], ba[1], wx[1], bx[1], lam[1], h0_b)
    return h_f + h_b[:, ::-1], s_f, s_b


def hybrid_mixer(h_lat, h_ctx, rows, w_in, lb, hg_g, rg_cw, rg_cb, rg, w_out, ctx_out):
    B = h_lat.shape[0]
    q_l, zf_l, zb_l, v_l, g_l, rx_l, rgate_l = split_proj(h_lat @ w_in)
    q_c, zf_c, zb_c, v_c, g_c, rx_c, rgate_c = split_proj(h_ctx @ w_in)
    q_l = jax.nn.silu(q_l)
    q_c = jax.nn.silu(q_c)

    zero_s = jnp.zeros((B, HG_HEADS, HG_DK, HG_DV), jnp.float32)
    o_c, s_f, s_b = hgrn2_bidirectional(q_c, zf_c, zb_c, v_c, lb, zero_s, zero_s)
    cm = lambda t: grid_to_colmajor(t, rows)
    o_l, _, _ = hgrn2_bidirectional(cm(q_l), cm(zf_l), cm(zb_l), cm(v_l), lb, s_f, s_b)
    o_l = colmajor_to_grid(o_l, rows)
    hg_l = hgrn2_readout(o_l, g_l, hg_g)

    zero_h = jnp.zeros((B, RG_WIDTH), jnp.float32)
    y_c, h_f, h_b = rglru_bidirectional(dwconv(rx_c, rg_cw, rg_cb, 2, 1), rg, zero_h, zero_h)
    y_l, _, _ = rglru_bidirectional(dwconv(rx_l, rg_cw, rg_cb, 2, 1), rg, h_f, h_b)
    rg_l = y_l * jax.nn.gelu(rgate_l)

    out_l = jnp.concatenate([hg_l, rg_l], axis=-1) @ w_out
    if not ctx_out:
        return out_l, None
    out_c = jnp.concatenate([hgrn2_readout(o_c, g_c, hg_g), y_c * jax.nn.gelu(rgate_c)], axis=-1) @ w_out
    return out_l, out_c


def conv_ffn(h, w_up, cw, cb, w_down):
    u = dwconv(h @ w_up, cw, cb, 1, 1)
    a, v = jnp.split(u, 2, axis=-1)
    return (jax.nn.gelu(a) * v) @ w_down


def setup_inputs(seed: int = 0) -> dict:
    key = jax.random.key(seed)
    ks = jax.random.split(key, 24)
    f32 = jnp.float32
    nrm = lambda k, shape, s: jax.random.normal(k, shape, f32) * s
    a0 = jax.random.uniform(ks[16], (DEPTH, 2, RG_WIDTH), f32, 0.9, 0.999)
    return {
        "x": nrm(ks[0], (BATCH, SEQ, D_MODEL), 1.0),
        "c": nrm(ks[1], (BATCH, D_MODEL), 1.0),
        "ctx": nrm(ks[2], (BATCH, CTX_LEN, D_MODEL), 1.0),
        "c_ctx": nrm(ks[3], (D_MODEL,), 1.0),
        "w_mod": nrm(ks[4], (DEPTH, D_MODEL, 6 * D_MODEL), 0.5 * D_MODEL ** -0.5),
        "b_mod": nrm(ks[5], (DEPTH, 6 * D_MODEL), 0.02),
        "norm_g": 1.0 + nrm(ks[6], (DEPTH, 4, D_MODEL), 0.05),
        "w_in": nrm(ks[7], (DEPTH, D_MODEL, N_PROJ), D_MODEL ** -0.5),
        "hg_lower_bounds": nrm(ks[8], (DEPTH, 2, HG_WIDTH), 0.1),
        "hg_norm_g": 1.0 + nrm(ks[9], (DEPTH, HG_WIDTH), 0.05),
        "rg_conv_w": nrm(ks[10], (DEPTH, RG_CONV, RG_WIDTH), RG_CONV ** -0.5),
        "rg_conv_b": nrm(ks[11], (DEPTH, RG_WIDTH), 0.02),
        "rg_wa": nrm(ks[12], (DEPTH, 2, RG_BLOCKS, RG_BW, RG_BW), RG_BW ** -0.5),
        "rg_ba": nrm(ks[13], (DEPTH, 2, RG_WIDTH), 0.1),
        "rg_wx": nrm(ks[14], (DEPTH, 2, RG_BLOCKS, RG_BW, RG_BW), RG_BW ** -0.5),
        "rg_bx": nrm(ks[15], (DEPTH, 2, RG_WIDTH), 0.1),
        "rg_lambda": jnp.log(a0) - jnp.log1p(-a0),
        "w_out": nrm(ks[17], (DEPTH, D_MIX, D_MODEL), D_MIX ** -0.5),
        "ffn_w_up": nrm(ks[18], (DEPTH, D_MODEL, 2 * D_FF), D_MODEL ** -0.5),
        "ffn_conv_w": nrm(ks[19], (DEPTH, FFN_CONV, 2 * D_FF), FFN_CONV ** -0.5),
        "ffn_conv_b": nrm(ks[20], (DEPTH, 2 * D_FF), 0.02),
        "ffn_w_down": nrm(ks[21], (DEPTH, D_FF, D_MODEL), D_FF ** -0.5),
    }


def reference(x, c, ctx, c_ctx, w_mod, b_mod, norm_g, w_in, hg_lower_bounds, hg_norm_g,
              rg_conv_w, rg_conv_b, rg_wa, rg_ba, rg_wx, rg_bx, rg_lambda, w_out,
              ffn_w_up, ffn_conv_w, ffn_conv_b, ffn_w_down):
    rows = x.shape[1] // GRID_W
    lb_all = jnp.cumsum(jax.nn.softmax(hg_lower_bounds.astype(jnp.float32), axis=0), axis=0)
    lb_all = lb_all - lb_all[0]
    for l in range(DEPTH):
        ctx_out = l < DEPTH - 1
        mod_l = jax.nn.silu(c) @ w_mod[l] + b_mod[l]
        mod_c = jax.nn.silu(c_ctx) @ w_mod[l] + b_mod[l]
        sh1, sc1, gt1, sh2, sc2, gt2 = jnp.split(mod_l[:, None, :], 6, axis=-1)
        csh1, csc1, cgt1, csh2, csc2, cgt2 = jnp.split(mod_c, 6, axis=-1)

        h_l = modulate(rmsnorm(x, norm_g[l, 0]), sh1, sc1)
        h_c = modulate(rmsnorm(ctx, norm_g[l, 0]), csh1, csc1)
        rg = (rg_wa[l], rg_ba[l], rg_wx[l], rg_bx[l], rg_lambda[l])
        mix_l, mix_c = hybrid_mixer(h_l, h_c, rows, w_in[l], lb_all[l], hg_norm_g[l],
                                    rg_conv_w[l], rg_conv_b[l], rg, w_out[l], ctx_out)
        x = x + gt1 * rmsnorm(mix_l, norm_g[l, 1])

        f_l = conv_ffn(modulate(rmsnorm(x, norm_g[l, 2]), sh2, sc2),
                       ffn_w_up[l], ffn_conv_w[l], ffn_conv_b[l], ffn_w_down[l])
        x = x + gt2 * rmsnorm(f_l, norm_g[l, 3])

        if ctx_out:
            ctx = ctx + cgt1 * rmsnorm(mix_c, norm_g[l, 1])
            f_c = conv_ffn(modulate(rmsnorm(ctx, norm_g[l, 2]), csh2, csc2),
                           ffn_w_up[l], ffn_conv_w[l], ffn_conv_b[l], ffn_w_down[l])
            ctx = ctx + cgt2 * rmsnorm(f_c, norm_g[l, 3])
    return x
```

```python
import functools
import math

import jax
import jax.numpy as jnp
from jax import lax
from jax.experimental import pallas as pl
from jax.experimental.pallas import tpu as pltpu

F32 = jnp.float32
BF16 = jnp.bfloat16

D_MODEL = 2048
BATCH = 4
SEQ = 4096
DEPTH = 4
GRID_W = 64
GRID_ROWS = SEQ // GRID_W
CTX_LEN = 256
HG_WIDTH = 1024
HG_HEADS = 8
HG_D = HG_WIDTH // HG_HEADS
CHUNK = 64
RG_WIDTH = 1024
RG_BLOCKS = 8
RG_BW = RG_WIDTH // RG_BLOCKS
RG_C = 8.0
D_FF = 5632
N_PROJ = 5 * HG_WIDTH + 2 * RG_WIDTH
EPS = 1e-6

T_LAT = BATCH * SEQ
T_CTX = BATCH * CTX_LEN
T_ALL = T_LAT + T_CTX
MOD_ROWS = 8
SUBLANES = 8
EXP_CLAMP = 80.0

VMEM_LIMIT = 56 * 1024 * 1024

COL_Q, COL_FF, COL_FB, COL_V, COL_G, COL_RX, COL_RGATE = range(7)

NT_DIMS = (((1,), (1,)), ((), ()))
TN_DIMS = (((0,), (0,)), ((), ()))


def _sigmoid(x):
    return 1.0 / (1.0 + jnp.exp(-x))


def _silu(x):
    return x * _sigmoid(x)


def _gelu_tanh(x):
    c = math.sqrt(2.0 / math.pi)
    return x * (0.5 * (1.0 + jnp.tanh(c * (x + 0.044715 * (x * x * x)))))


def _rms_scale(x):
    return lax.rsqrt(jnp.mean(x * x, axis=-1, keepdims=True) + EPS)


def _params(*sem):
    return pltpu.CompilerParams(dimension_semantics=sem, vmem_limit_bytes=VMEM_LIMIT)


def _mod_row(i, tm):
    return jnp.minimum(i // (SEQ // tm), BATCH)


def _lb_kernel(x_ref, o_ref):
    x = x_ref[...]
    m = jnp.max(x, axis=0, keepdims=True)
    e = jnp.exp(x - m)
    sm = e / jnp.sum(e, axis=0, keepdims=True)
    acc = jnp.zeros_like(sm[0:1])
    o_ref[0:1, :] = acc
    for l in range(1, DEPTH):
        acc = acc + sm[l:l + 1]
        o_ref[l:l + 1, :] = acc


def _lower_bounds(hg_lower_bounds):
    x = hg_lower_bounds.reshape(DEPTH, 2 * HG_WIDTH)
    out = pl.pallas_call(
        _lb_kernel,
        out_shape=jax.ShapeDtypeStruct((DEPTH, 2 * HG_WIDTH), F32),
        name="hg_lower_bounds",
    )(x)
    return out.reshape(DEPTH, 2, HG_WIDTH)


def _mod_kernel(cc_ref, w_ref, b_ref, o_ref):
    s = _silu(cc_ref[...])
    o_ref[...] = jnp.dot(s.astype(BF16), w_ref[...].astype(BF16),
                         preferred_element_type=F32) + b_ref[...]


def _modulation(cc, w_mod, b_mod):
    tn = 1024
    n_out = 6 * D_MODEL
    out = pl.pallas_call(
        _mod_kernel,
        grid=(DEPTH, n_out // tn),
        in_specs=[
            pl.BlockSpec((MOD_ROWS, D_MODEL), lambda l, n: (0, 0)),
            pl.BlockSpec((None, D_MODEL, tn), lambda l, n: (l, 0, n)),
            pl.BlockSpec((None, 1, tn), lambda l, n: (l, 0, n)),
        ],
        out_specs=pl.BlockSpec((None, MOD_ROWS, tn), lambda l, n: (l, 0, n)),
        out_shape=jax.ShapeDtypeStruct((DEPTH, MOD_ROWS, n_out), F32),
        compiler_params=_params("parallel", "parallel"),
        name="adaln_modulation",
    )(cc, w_mod, b_mod.reshape(DEPTH, 1, n_out))
    return out.reshape(DEPTH, MOD_ROWS, 6, D_MODEL)


def _inproj_kernel(x_ref, mod_ref, g_ref, lb_ref, w_ref, o_ref, h_ref):
    n = pl.program_id(1)

    @pl.when(n == 0)
    def _():
        x = x_ref[...]
        y = x * _rms_scale(x) * g_ref[0:1, :]
        h_ref[...] = (y * (1.0 + mod_ref[1:2, :]) + mod_ref[0:1, :]).astype(BF16)

    acc = jnp.dot(h_ref[...], w_ref[...], preferred_element_type=F32)

    @pl.when((n == COL_Q) | (n == COL_G))
    def _():
        o_ref[...] = _silu(acc)

    @pl.when(n == COL_FF)
    def _():
        lb = lb_ref[0:1, :]
        o_ref[...] = lb + (1.0 - lb) * _sigmoid(acc)

    @pl.when(n == COL_FB)
    def _():
        lb = lb_ref[1:2, :]
        o_ref[...] = lb + (1.0 - lb) * _sigmoid(acc)

    @pl.when((n == COL_V) | (n == COL_RX))
    def _():
        o_ref[...] = acc

    @pl.when(n == COL_RGATE)
    def _():
        o_ref[...] = _gelu_tanh(acc)


def _inproj(x_all, mod_l, norm_g_l, lb_l, w_in_l):
    tm, tn = 1024, 1024
    return pl.pallas_call(
        _inproj_kernel,
        grid=(T_ALL // tm, N_PROJ // tn),
        in_specs=[
            pl.BlockSpec((tm, D_MODEL), lambda i, n: (i, 0)),
            pl.BlockSpec((None, 6, D_MODEL), lambda i, n: (_mod_row(i, tm), 0, 0)),
            pl.BlockSpec((4, D_MODEL), lambda i, n: (0, 0)),
            pl.BlockSpec((2, HG_WIDTH), lambda i, n: (0, 0)),
            pl.BlockSpec((D_MODEL, tn), lambda i, n: (0, n)),
        ],
        out_specs=pl.BlockSpec((tm, tn), lambda i, n: (i, n)),
        out_shape=jax.ShapeDtypeStruct((T_ALL, N_PROJ), F32),
        scratch_shapes=[pltpu.VMEM((tm, D_MODEL), BF16)],
        compiler_params=_params("parallel", "arbitrary"),
        name="in_projection",
    )(x_all, mod_l, norm_g_l, lb_l, w_in_l)


def _hgrn_kernel(*refs, rev, readout, aliased):
    refs = list(refs)
    q_ref, f_ref, v_ref, s0_ref = refs[:4]
    pos = 4
    if readout:
        ob_ref, sg_ref, hgg_ref = refs[pos:pos + 3]
        pos += 3
    if aliased:
        pos += 1
    o_ref, sfin_ref, s_scr = refs[pos:pos + 3]

    i = pl.program_id(1)
    C = CHUNK

    @pl.when(i == 0)
    def _():
        s_scr[...] = s0_ref[...]

    q = q_ref[...]
    f = f_ref[...]
    v = v_ref[...]
    k = 1.0 - f
    g = jnp.log(f)

    def scan_pos(idx):
        return (C - 1 - idx) if rev else idx

    pr = scan_pos(lax.broadcasted_iota(jnp.int32, (C, C), 0))
    pc = scan_pos(lax.broadcasted_iota(jnp.int32, (C, C), 1))
    tri = (pc <= pr).astype(F32)
    b = jnp.dot(tri, g, precision=lax.Precision.HIGHEST, preferred_element_type=F32)

    p = scan_pos(lax.broadcasted_iota(jnp.int32, (C, HG_WIDTH), 0))

    def brow(sp):
        r = scan_pos(sp)
        return b[r:r + 1, :]

    blk = p >> 4
    centre = [0.5 * (brow(16 * j) + brow(16 * j + 15)) for j in range(4)]
    ref0 = jnp.where(blk == 0, centre[0], jnp.where(blk == 1, centre[1],
                     jnp.where(blk == 2, centre[2], centre[3])))
    d0 = b - ref0
    q0 = (q * jnp.exp(jnp.clip(d0, -EXP_CLAMP, EXP_CLAMP))).astype(BF16)
    k0 = (k * jnp.exp(jnp.clip(-d0, -EXP_CLAMP, EXP_CLAMP))).astype(BF16)
    odd = (blk & 1) == 1
    d1 = b - jnp.where(p < 32, brow(15), brow(47))
    q1 = jnp.where(odd, q * jnp.exp(jnp.minimum(d1, 0.0)), 0.0).astype(BF16)
    k1 = jnp.where(odd, 0.0, k * jnp.exp(jnp.minimum(-d1, 0.0))).astype(BF16)
    hi = p >= 32
    d2 = b - brow(31)
    q2 = jnp.where(hi, q * jnp.exp(jnp.minimum(d2, 0.0)), 0.0).astype(BF16)
    k2 = jnp.where(hi, 0.0, k * jnp.exp(jnp.minimum(-d2, 0.0))).astype(BF16)
    b_last = brow(C - 1)
    q3 = (q * jnp.exp(b)).astype(BF16)
    k3 = (k * jnp.exp(b_last - b)).astype(BF16)
    dec = jnp.exp(b_last)
    vb = v.astype(BF16)

    m0 = ((pr >> 4) == (pc >> 4)) & (pc <= pr)
    m1 = ((pr >> 5) == (pc >> 5)) & (((pr >> 4) & 1) == 1) & (((pc >> 4) & 1) == 0)

    for h in range(HG_HEADS):
        sl = slice(h * HG_D, (h + 1) * HG_D)
        a0 = lax.dot_general(q0[:, sl], k0[:, sl], NT_DIMS, preferred_element_type=F32)
        a1 = lax.dot_general(q1[:, sl], k1[:, sl], NT_DIMS, preferred_element_type=F32)
        a2 = lax.dot_general(q2[:, sl], k2[:, sl], NT_DIMS, preferred_element_type=F32)
        scores = jnp.where(m0, a0, jnp.where(m1, a1, a2))
        st = s_scr[h]
        o_h = (jnp.dot(scores.astype(BF16), vb[:, sl], preferred_element_type=F32)
               + lax.dot_general(q3[:, sl], st.astype(BF16), NT_DIMS, preferred_element_type=F32))
        upd = lax.dot_general(vb[:, sl], k3[:, sl], TN_DIMS, preferred_element_type=F32)
        s_scr[h] = st * dec[:, sl] + upd
        if readout:
            o_t = o_h + ob_ref[:, sl]
            y = o_t * _rms_scale(o_t) * hgg_ref[:, sl] * sg_ref[:, sl]
            o_ref[:, sl] = y.astype(o_ref.dtype)
        else:
            o_ref[:, sl] = o_h

    @pl.when(i == pl.num_programs(1) - 1)
    def _():
        sfin_ref[...] = s_scr[...]


def _hgrn_call(p_all, s0, lb_dir_col, *, latent, rev, o_back=None, hgg=None, out_buf=None):
    readout = o_back is not None
    aliased = out_buf is not None
    n_chunks = (SEQ if latent else CTX_LEN) // CHUNK
    ctx_row0 = T_LAT // CHUNK
    out_dtype = BF16 if readout else F32

    def chunk(i):
        return (n_chunks - 1 - i) if rev else i

    if latent:
        p_view = p_all.reshape(T_ALL // GRID_ROWS, GRID_ROWS * N_PROJ)
        cols_per_tok = N_PROJ // HG_WIDTH

        def pspec(col):
            return pl.BlockSpec((CHUNK, HG_WIDTH), lambda b, i: (b, chunk(i) * cols_per_tok + col))

        def ospec():
            return pl.BlockSpec((CHUNK, HG_WIDTH), lambda b, i: (b, chunk(i)))

        o_shape = (T_ALL // GRID_ROWS, GRID_ROWS * HG_WIDTH)
    else:
        p_view = p_all

        def pspec(col):
            return pl.BlockSpec((CHUNK, HG_WIDTH), lambda b, i: (ctx_row0 + b * n_chunks + chunk(i), col))

        def ospec():
            return pl.BlockSpec((CHUNK, HG_WIDTH), lambda b, i: (ctx_row0 + b * n_chunks + chunk(i), 0))

        o_shape = (T_ALL, HG_WIDTH)

    state_spec = pl.BlockSpec((None, HG_HEADS, HG_D, HG_D), lambda b, i: (b, 0, 0, 0))
    in_specs = [pspec(COL_Q), pspec(lb_dir_col), pspec(COL_V), state_spec]
    args = [p_view, p_view, p_view, s0]
    if readout:
        in_specs += [ospec(), pspec(COL_G), pl.BlockSpec((1, HG_WIDTH), lambda b, i: (0, 0))]
        args += [o_back.reshape(o_shape), p_view, hgg]
    aliases = {}
    if aliased:
        in_specs.append(pl.BlockSpec(memory_space=pl.ANY))
        args.append(out_buf.reshape(o_shape))
        aliases = {len(args) - 1: 0}

    o, s_fin = pl.pallas_call(
        functools.partial(_hgrn_kernel, rev=rev, readout=readout, aliased=aliased),
        grid=(BATCH, n_chunks),
        in_specs=in_specs,
        out_specs=[ospec(), state_spec],
        out_shape=[jax.ShapeDtypeStruct(o_shape, out_dtype),
                   jax.ShapeDtypeStruct((BATCH, HG_HEADS, HG_D, HG_D), F32)],
        scratch_shapes=[pltpu.VMEM((HG_HEADS, HG_D, HG_D), F32)],
        input_output_aliases=aliases,
        compiler_params=_params("parallel", "arbitrary"),
        name="hgrn2_%s_%s" % ("latent" if latent else "context", "bwd" if rev else "fwd"),
    )(*args)
    return o.reshape(T_ALL, HG_WIDTH), s_fin


def _hgrn_mixer(p_all, hgg):
    zero_s = jnp.zeros((BATCH, HG_HEADS, HG_D, HG_D), F32)
    o_b, s_b = _hgrn_call(p_all, zero_s, COL_FB, latent=False, rev=True)
    o_b, _ = _hgrn_call(p_all, s_b, COL_FB, latent=True, rev=True, out_buf=o_b)
    hg, s_f = _hgrn_call(p_all, zero_s, COL_FF, latent=False, rev=False, o_back=o_b, hgg=hgg)
    hg, _ = _hgrn_call(p_all, s_f, COL_FF, latent=True, rev=False, o_back=o_b, hgg=hgg, out_buf=hg)
    return hg


RG_ROWS = 256
RG_NB = SEQ // RG_ROWS


def _rg_block(j, rev):
    blk = (RG_NB - j) if rev else (j - 1)
    return jnp.clip(blk, 0, RG_NB - 1)


def _rg_rowblock(b, j, rev):
    return jnp.where(j == 0, T_LAT // RG_ROWS + b, b * RG_NB + _rg_block(j, rev))


def _rg_kernel(*refs, rev, final):
    xp_ref, xm_ref, xn_ref, cw_ref, cb_ref, wa_ref, ba_ref, wx_ref, bx_ref, lam_ref = refs[:10]
    if final:
        hb_ref, gate_ref, o_ref, carry = refs[10:]
    else:
        o_ref, carry = refs[10:]
    R = RG_ROWS
    N = R + 2 * SUBLANES
    j = pl.program_id(1)
    blk = _rg_block(j, rev)
    is_ctx = j == 0
    first = is_ctx | (blk == 0)
    last = is_ctx | (blk == RG_NB - 1)

    @pl.when(j == 0)
    def _():
        carry[...] = jnp.zeros_like(carry)

    xm = xm_ref[...]
    xp = jnp.where(first, 0.0, xp_ref[...])
    xn = jnp.where(last, 0.0, xn_ref[...])
    ext = jnp.concatenate([xp, xm, xn], axis=0)
    lo, hi_ = SUBLANES, SUBLANES + R
    xc = (cw_ref[0:1, :] * pltpu.roll(ext, 2, 0)[lo:hi_]
          + cw_ref[1:2, :] * pltpu.roll(ext, 1, 0)[lo:hi_]
          + cw_ref[2:3, :] * xm
          + cw_ref[3:4, :] * pltpu.roll(ext, N - 1, 0)[lo:hi_]
          + cb_ref[...])

    xcb = xc.astype(BF16)
    r_parts, i_parts = [], []
    for n in range(RG_BLOCKS):
        sl = slice(n * RG_BW, (n + 1) * RG_BW)
        r_parts.append(jnp.dot(xcb[:, sl], wa_ref[n], preferred_element_type=F32))
        i_parts.append(jnp.dot(xcb[:, sl], wx_ref[n], preferred_element_type=F32))
    r = _sigmoid(jnp.concatenate(r_parts, axis=1) + ba_ref[...])
    ig = _sigmoid(jnp.concatenate(i_parts, axis=1) + bx_ref[...])
    nl = -lam_ref[...]
    softplus = jnp.maximum(nl, 0.0) + jnp.log1p(jnp.exp(-jnp.abs(nl)))
    log_a = (-RG_C) * r * softplus
    a = jnp.exp(log_a)
    bt = jnp.sqrt(-jnp.tanh(log_a) * (1.0 + a * a)) * ig * xc

    row = lax.broadcasted_iota(jnp.int32, (R, RG_WIDTH), 0)
    step = 1
    while step < R:
        if rev:
            keep = row < R - step
            shift = R - step
        else:
            keep = row >= step
            shift = step
        a_sh = jnp.where(keep, pltpu.roll(a, shift, 0), 1.0)
        b_sh = jnp.where(keep, pltpu.roll(bt, shift, 0), 0.0)
        bt = bt + a * b_sh
        a = a * a_sh
        step *= 2
    h = bt + a * carry[...]
    end = 0 if rev else R - 1
    carry[...] = h[end:end + 1, :]

    if final:
        o_ref[...] = ((h + hb_ref[...]) * gate_ref[...]).astype(o_ref.dtype)
    else:
        o_ref[...] = h


def _rg_call(p_all, cw, cb, wa, ba, wx, bx, lam, *, rev, h_back=None):
    final = h_back is not None
    R = RG_ROWS
    per8 = R // SUBLANES
    last8 = T_ALL // SUBLANES - 1

    def main(col):
        return pl.BlockSpec((R, RG_WIDTH), lambda b, j: (_rg_rowblock(b, j, rev), col))

    def halo(offset):
        def idx(b, j):
            rb = _rg_rowblock(b, j, rev)
            return (jnp.clip(rb * per8 + offset, 0, last8), COL_RX)
        return pl.BlockSpec((SUBLANES, RG_WIDTH), idx)

    def const(shape):
        return pl.BlockSpec(shape, lambda b, j: (0,) * len(shape))

    in_specs = [halo(-1), main(COL_RX), halo(per8),
                const((4, RG_WIDTH)), const((1, RG_WIDTH)),
                const((RG_BLOCKS, RG_BW, RG_BW)), const((1, RG_WIDTH)),
                const((RG_BLOCKS, RG_BW, RG_BW)), const((1, RG_WIDTH)), const((1, RG_WIDTH))]
    args = [p_all, p_all, p_all, cw, cb, wa, ba, wx, bx, lam]
    if final:
        in_specs += [main(0), main(COL_RGATE)]
        args += [h_back, p_all]
    return pl.pallas_call(
        functools.partial(_rg_kernel, rev=rev, final=final),
        grid=(BATCH, 1 + RG_NB),
        in_specs=in_specs,
        out_specs=main(0),
        out_shape=jax.ShapeDtypeStruct((T_ALL, RG_WIDTH), BF16 if final else F32),
        scratch_shapes=[pltpu.VMEM((1, RG_WIDTH), F32)],
        compiler_params=_params("parallel", "arbitrary"),
        name="rglru_%s" % ("bwd" if rev else "fwd"),
    )(*args)


def _rg_mixer(p_all, cw, cb, wa, ba, wx, bx, lam):
    row = lambda t: t.reshape(1, RG_WIDTH)
    h_b = _rg_call(p_all, cw, row(cb), wa[1].astype(BF16), row(ba[1]), wx[1].astype(BF16),
                   row(bx[1]), row(lam[1]), rev=True)
    return _rg_call(p_all, cw, row(cb), wa[0].astype(BF16), row(ba[0]), wx[0].astype(BF16),
                    row(bx[0]), row(lam[0]), rev=False, h_back=h_b)


def _outproj_kernel(hg_ref, rg_ref, w_ref, x_ref, mod_ref, g_ref, o_ref):
    mix = (jnp.dot(hg_ref[...], w_ref[0:HG_WIDTH, :], preferred_element_type=F32)
           + jnp.dot(rg_ref[...], w_ref[HG_WIDTH:, :], preferred_element_type=F32))
    y = mix * _rms_scale(mix) * g_ref[1:2, :]
    o_ref[...] = x_ref[...] + mod_ref[2:3, :] * y


def _outproj(hg, rg, w_out_l, x_all, mod_l, norm_g_l, n_rows):
    tm = 512
    return pl.pallas_call(
        _outproj_kernel,
        grid=(n_rows // tm,),
        in_specs=[
            pl.BlockSpec((tm, HG_WIDTH), lambda i: (i, 0)),
            pl.BlockSpec((tm, RG_WIDTH), lambda i: (i, 0)),
            pl.BlockSpec((D_MODEL, D_MODEL), lambda i: (0, 0)),
            pl.BlockSpec((tm, D_MODEL), lambda i: (i, 0)),
            pl.BlockSpec((None, 6, D_MODEL), lambda i: (_mod_row(i, tm), 0, 0)),
            pl.BlockSpec((4, D_MODEL), lambda i: (0, 0)),
        ],
        out_specs=pl.BlockSpec((tm, D_MODEL), lambda i: (i, 0)),
        out_shape=jax.ShapeDtypeStruct((T_ALL, D_MODEL), F32),
        input_output_aliases={3: 0},
        compiler_params=_params("parallel"),
        name="out_projection",
    )(hg, rg, w_out_l, x_all, mod_l, norm_g_l)


def _ffn_up_kernel(xp_ref, xm_ref, xn_ref, mod_ref, g_ref, wa_ref, wv_ref, cwa_ref, cwv_ref,
                   cba_ref, cbv_ref, o_ref, h_ref, *, tm):
    i = pl.program_id(0)
    n = pl.program_id(1)
    H = SUBLANES
    N = tm + 2 * H

    @pl.when(n == 0)
    def _():
        def norm_mod(x):
            y = x * _rms_scale(x) * g_ref[2:3, :]
            return (y * (1.0 + mod_ref[4:5, :]) + mod_ref[3:4, :]).astype(BF16)
        h_ref[0:H, :] = norm_mod(xp_ref[...])
        h_ref[H:H + tm, :] = norm_mod(xm_ref[...])
        h_ref[H + tm:N, :] = norm_mod(xn_ref[...])

    row = i * tm + lax.broadcasted_iota(jnp.int32, (tm, 1), 0)
    is_lat = row < T_LAT
    pos = jnp.where(is_lat, row % SEQ, (row - T_LAT) % CTX_LEN)
    seq_len = jnp.where(is_lat, SEQ, CTX_LEN)
    has_prev = (pos != 0).astype(F32)
    has_next = (pos != seq_len - 1).astype(F32)

    def conv(u, cw_ref, cb_ref):
        return (cw_ref[0:1, :] * (pltpu.roll(u, 1, 0)[H:H + tm] * has_prev)
                + cw_ref[1:2, :] * u[H:H + tm]
                + cw_ref[2:3, :] * (pltpu.roll(u, N - 1, 0)[H:H + tm] * has_next)
                + cb_ref[...])

    hh = h_ref[...]
    ua = conv(jnp.dot(hh, wa_ref[...], preferred_element_type=F32), cwa_ref, cba_ref)
    uv = conv(jnp.dot(hh, wv_ref[...], preferred_element_type=F32), cwv_ref, cbv_ref)
    o_ref[...] = (_gelu_tanh(ua) * uv).astype(o_ref.dtype)


def _ffn_up(x_all, mod_l, norm_g_l, w_up_l, cw_l, cb_l, n_rows):
    tm, tn = 1024, 512
    nn = D_FF // tn
    per8 = tm // SUBLANES
    last8 = T_ALL // SUBLANES - 1
    cb2 = cb_l.reshape(1, 2 * D_FF)
    return pl.pallas_call(
        functools.partial(_ffn_up_kernel, tm=tm),
        grid=(n_rows // tm, nn),
        in_specs=[
            pl.BlockSpec((SUBLANES, D_MODEL), lambda i, n: (jnp.maximum(i * per8 - 1, 0), 0)),
            pl.BlockSpec((tm, D_MODEL), lambda i, n: (i, 0)),
            pl.BlockSpec((SUBLANES, D_MODEL), lambda i, n: (jnp.minimum((i + 1) * per8, last8), 0)),
            pl.BlockSpec((None, 6, D_MODEL), lambda i, n: (_mod_row(i, tm), 0, 0)),
            pl.BlockSpec((4, D_MODEL), lambda i, n: (0, 0)),
            pl.BlockSpec((D_MODEL, tn), lambda i, n: (0, n)),
            pl.BlockSpec((D_MODEL, tn), lambda i, n: (0, nn + n)),
            pl.BlockSpec((3, tn), lambda i, n: (0, n)),
            pl.BlockSpec((3, tn), lambda i, n: (0, nn + n)),
            pl.BlockSpec((1, tn), lambda i, n: (0, n)),
            pl.BlockSpec((1, tn), lambda i, n: (0, nn + n)),
        ],
        out_specs=pl.BlockSpec((tm, tn), lambda i, n: (i, n)),
        out_shape=jax.ShapeDtypeStruct((n_rows, D_FF), BF16),
        scratch_shapes=[pltpu.VMEM((tm + 2 * SUBLANES, D_MODEL), BF16)],
        compiler_params=_params("parallel", "arbitrary"),
        name="ffn_up_conv_geglu",
    )(x_all, x_all, x_all, mod_l, norm_g_l, w_up_l, w_up_l, cw_l, cw_l, cb2, cb2)


def _ffn_down_kernel(a_ref, w_ref, x_ref, mod_ref, g_ref, o_ref, acc_ref):
    k = pl.program_id(1)

    @pl.when(k == 0)
    def _():
        acc_ref[...] = jnp.zeros_like(acc_ref)

    acc_ref[...] += jnp.dot(a_ref[...], w_ref[...], preferred_element_type=F32)

    @pl.when(k == pl.num_programs(1) - 1)
    def _():
        f = acc_ref[...]
        y = f * _rms_scale(f) * g_ref[3:4, :]
        o_ref[...] = x_ref[...] + mod_ref[5:6, :] * y


def _ffn_down(act, w_down_l, x_all, mod_l, norm_g_l, n_rows, in_place):
    tm, tk = 512, 1408
    return pl.pallas_call(
        _ffn_down_kernel,
        grid=(n_rows // tm, D_FF // tk),
        in_specs=[
            pl.BlockSpec((tm, tk), lambda i, k: (i, k)),
            pl.BlockSpec((tk, D_MODEL), lambda i, k: (k, 0)),
            pl.BlockSpec((tm, D_MODEL), lambda i, k: (i, 0)),
            pl.BlockSpec((None, 6, D_MODEL), lambda i, k: (_mod_row(i, tm), 0, 0)),
            pl.BlockSpec((4, D_MODEL), lambda i, k: (0, 0)),
        ],
        out_specs=pl.BlockSpec((tm, D_MODEL), lambda i, k: (i, 0)),
        out_shape=jax.ShapeDtypeStruct((T_ALL if in_place else n_rows, D_MODEL), F32),
        scratch_shapes=[pltpu.VMEM((tm, D_MODEL), F32)],
        input_output_aliases={2: 0} if in_place else {},
        compiler_params=_params("parallel", "arbitrary"),
        name="ffn_down",
    )(act, w_down_l, x_all, mod_l, norm_g_l)


def kernel(x, c, ctx, c_ctx, w_mod, b_mod, norm_g, w_in, hg_lower_bounds, hg_norm_g, rg_conv_w,
           rg_conv_b, rg_wa, rg_ba, rg_wx, rg_bx, rg_lambda, w_out, ffn_w_up, ffn_conv_w,
           ffn_conv_b, ffn_w_down):
    x_all = jnp.concatenate([x.reshape(T_LAT, D_MODEL), ctx.reshape(T_CTX, D_MODEL)], axis=0)
    cc = jnp.concatenate([c, c_ctx[None, :], jnp.zeros((MOD_ROWS - BATCH - 1, D_MODEL), F32)], axis=0)
    mods = _modulation(cc, w_mod, b_mod)
    lb_all = _lower_bounds(hg_lower_bounds)

    for l in range(DEPTH):
        ctx_out = l < DEPTH - 1
        n_rows = T_ALL if ctx_out else T_LAT
        p_all = _inproj(x_all, mods[l], norm_g[l], lb_all[l], w_in[l].astype(BF16))
        hg = _hgrn_mixer(p_all, hg_norm_g[l].reshape(1, HG_WIDTH))
        rg = _rg_mixer(p_all, rg_conv_w[l], rg_conv_b[l], rg_wa[l], rg_ba[l], rg_wx[l], rg_bx[l],
                       rg_lambda[l])
        x_all = _outproj(hg, rg, w_out[l].astype(BF16), x_all, mods[l], norm_g[l], n_rows)
        act = _ffn_up(x_all, mods[l], norm_g[l], ffn_w_up[l].astype(BF16), ffn_conv_w[l],
                      ffn_conv_b[l], n_rows)
        x_all = _ffn_down(act, ffn_w_down[l].astype(BF16), x_all, mods[l], norm_g[l], n_rows,
                          in_place=ctx_out)
    return x_all.reshape(BATCH, SEQ, D_MODEL)
```

```python
import functools
import math

import jax
import jax.numpy as jnp
from jax import lax
from jax.experimental import pallas as pl
from jax.experimental.pallas import tpu as pltpu

F32 = jnp.float32
BF16 = jnp.bfloat16

D_MODEL = 2048
BATCH = 4
SEQ = 4096
DEPTH = 4
GRID_W = 64
GRID_ROWS = SEQ // GRID_W
CTX_LEN = 256
HG_WIDTH = 1024
HG_HEADS = 8
HG_D = HG_WIDTH // HG_HEADS
CHUNK = 64
RG_WIDTH = 1024
RG_BLOCKS = 8
RG_BW = RG_WIDTH // RG_BLOCKS
RG_C = 8.0
D_FF = 5632
N_PROJ = 5 * HG_WIDTH + 2 * RG_WIDTH
EPS = 1e-6

T_LAT = BATCH * SEQ
T_CTX = BATCH * CTX_LEN
T_ALL = T_LAT + T_CTX
MOD_ROWS = 8
SUBLANES = 8
SLAB = 16
BAND = SLAB * GRID_W
BANDS = SEQ // BAND
EXP_CLAMP = 80.0

VMEM_LIMIT = 56 * 1024 * 1024

COL_Q, COL_FF, COL_FB, COL_V, COL_G, COL_RX, COL_RGATE = range(7)

NT_DIMS = (((1,), (1,)), ((), ()))
TN_DIMS = (((0,), (0,)), ((), ()))


def _sigmoid(x):
    return 0.5 + 0.5 * jnp.tanh(0.5 * x)


def _silu(x):
    return x * _sigmoid(x)


def _gelu_tanh(x):
    c = math.sqrt(2.0 / math.pi)
    return x * (0.5 * (1.0 + jnp.tanh(c * (x + 0.044715 * (x * x * x)))))


def _rms_scale(x):
    return lax.rsqrt(jnp.mean(x * x, axis=-1, keepdims=True) + EPS)


def _params(*sem):
    return pltpu.CompilerParams(dimension_semantics=sem, vmem_limit_bytes=VMEM_LIMIT)


def _mod_row(i, tm):
    return jnp.minimum(i // (SEQ // tm), BATCH)


def _to_banded(x):
    t = x.reshape(BATCH, BANDS, SLAB, GRID_W, D_MODEL)
    return t.transpose(0, 1, 3, 2, 4).reshape(T_LAT, D_MODEL)


def _from_banded(y):
    t = y.reshape(BATCH, BANDS, GRID_W, SLAB, D_MODEL)
    return t.transpose(0, 1, 3, 2, 4).reshape(BATCH, SEQ, D_MODEL)


def _lb_kernel(x_ref, o_ref):
    x = x_ref[...]
    m = jnp.max(x, axis=0, keepdims=True)
    e = jnp.exp(x - m)
    sm = e / jnp.sum(e, axis=0, keepdims=True)
    acc = jnp.zeros_like(sm[0:1])
    o_ref[0:1, :] = acc
    for l in range(1, DEPTH):
        acc = acc + sm[l:l + 1]
        o_ref[l:l + 1, :] = acc


def _lower_bounds(hg_lower_bounds):
    x = hg_lower_bounds.reshape(DEPTH, 2 * HG_WIDTH)
    out = pl.pallas_call(
        _lb_kernel,
        out_shape=jax.ShapeDtypeStruct((DEPTH, 2 * HG_WIDTH), F32),
        name="hg_lower_bounds",
    )(x)
    return out.reshape(DEPTH, 2, HG_WIDTH)


def _mod_kernel(cc_ref, w_ref, b_ref, o_ref):
    s = _silu(cc_ref[...])
    o_ref[...] = jnp.dot(s.astype(BF16), w_ref[...].astype(BF16),
                         preferred_element_type=F32) + b_ref[...]


def _modulation(cc, w_mod, b_mod):
    tn = 1024
    n_out = 6 * D_MODEL
    out = pl.pallas_call(
        _mod_kernel,
        grid=(DEPTH, n_out // tn),
        in_specs=[
            pl.BlockSpec((MOD_ROWS, D_MODEL), lambda l, n: (0, 0)),
            pl.BlockSpec((None, D_MODEL, tn), lambda l, n: (l, 0, n)),
            pl.BlockSpec((None, 1, tn), lambda l, n: (l, 0, n)),
        ],
        out_specs=pl.BlockSpec((None, MOD_ROWS, tn), lambda l, n: (l, 0, n)),
        out_shape=jax.ShapeDtypeStruct((DEPTH, MOD_ROWS, n_out), F32),
        compiler_params=_params("parallel", "parallel"),
        name="adaln_modulation",
    )(cc, w_mod, b_mod.reshape(DEPTH, 1, n_out))
    return out.reshape(DEPTH, MOD_ROWS, 6, D_MODEL)


def _inproj_kernel(x_ref, mod_ref, g_ref, lb_ref, w_ref, o_ref):
    x = x_ref[...]
    y = x * _rms_scale(x) * g_ref[0:1, :]
    h = (y * (1.0 + mod_ref[1:2, :]) + mod_ref[0:1, :]).astype(BF16)

    def gate(lb):
        return lambda z: lb + (1.0 - lb) * _sigmoid(z)

    acts = {COL_Q: _silu, COL_FF: gate(lb_ref[0:1, :]), COL_FB: gate(lb_ref[1:2, :]), COL_V: None,
            COL_G: _silu, COL_RX: None, COL_RGATE: _gelu_tanh}
    for col, act in acts.items():
        cs = slice(col * HG_WIDTH, (col + 1) * HG_WIDTH)
        z = jnp.dot(h, w_ref[:, cs], preferred_element_type=F32)
        o_ref[:, cs] = z if act is None else act(z)


def _inproj(x_all, mod_l, norm_g_l, lb_l, w_in_l):
    tm = 256
    return pl.pallas_call(
        _inproj_kernel,
        grid=(T_ALL // tm,),
        in_specs=[
            pl.BlockSpec((tm, D_MODEL), lambda i: (i, 0)),
            pl.BlockSpec((None, 6, D_MODEL), lambda i: (_mod_row(i, tm), 0, 0)),
            pl.BlockSpec((4, D_MODEL), lambda i: (0, 0)),
            pl.BlockSpec((2, HG_WIDTH), lambda i: (0, 0)),
            pl.BlockSpec((D_MODEL, N_PROJ), lambda i: (0, 0), pipeline_mode=pl.Buffered(1)),
        ],
        out_specs=pl.BlockSpec((tm, N_PROJ), lambda i: (i, 0)),
        out_shape=jax.ShapeDtypeStruct((T_ALL, N_PROJ), F32),
        compiler_params=_params("parallel"),
        name="in_projection",
    )(x_all, mod_l, norm_g_l, lb_l, w_in_l)


def _hgrn_kernel(*refs, rev, readout, aliased):
    refs = list(refs)
    q_ref, f_ref, v_ref, s0_ref = refs[:4]
    pos = 4
    if readout:
        ob_ref, sg_ref, hgg_ref = refs[pos:pos + 3]
        pos += 3
    if aliased:
        pos += 1
    o_ref, sfin_ref, s_scr = refs[pos:pos + 3]

    i = pl.program_id(1)
    C = CHUNK
    NB = C // 16

    @pl.when(i == 0)
    def _():
        s_scr[...] = s0_ref[...]

    def chunk_rows(ref):
        return ref[...].reshape(C, HG_WIDTH)

    q = chunk_rows(q_ref)
    f = chunk_rows(f_ref)
    vb = chunk_rows(v_ref).astype(BF16)
    k = 1.0 - f
    g = jnp.log(f)

    def scan_pos(idx):
        return (C - 1 - idx) if rev else idx

    pr = scan_pos(lax.broadcasted_iota(jnp.int32, (C, C), 0))
    pc = scan_pos(lax.broadcasted_iota(jnp.int32, (C, C), 1))
    tri = (pc <= pr).astype(F32)
    b = jnp.dot(tri, g, precision=lax.Precision.HIGHEST, preferred_element_type=F32)

    def rows(p0, p1):
        return slice(C - p1, C - p0) if rev else slice(p0, p1)

    def brow(sp):
        r = scan_pos(sp)
        return b[r:r + 1, :]

    def assemble(blocks):
        zero = jnp.zeros((16, HG_WIDTH), F32)
        blocks = [zero if blk is None else blk for blk in blocks]
        if rev:
            blocks = blocks[::-1]
        return jnp.concatenate(blocks, axis=0).astype(BF16)

    q0b, k0b = [], []
    for j in range(NB):
        R = rows(16 * j, 16 * j + 16)
        d = b[R] - 0.5 * (brow(16 * j) + brow(16 * j + 15))
        q0b.append(q[R] * jnp.exp(jnp.clip(d, -EXP_CLAMP, EXP_CLAMP)))
        k0b.append(k[R] * jnp.exp(jnp.clip(-d, -EXP_CLAMP, EXP_CLAMP)))
    q0, k0 = assemble(q0b), assemble(k0b)
    q1b, k1b = [], []
    for m in range(NB // 2):
        ref = brow(32 * m + 15)
        Re, Ro = rows(32 * m, 32 * m + 16), rows(32 * m + 16, 32 * m + 32)
        k1b += [k[Re] * jnp.exp(ref - b[Re]), None]
        q1b += [None, q[Ro] * jnp.exp(b[Ro] - ref)]
    q1, k1 = assemble(q1b), assemble(k1b)
    ref = brow(31)
    q2b = [None, None] + [q[rows(16 * j, 16 * j + 16)] * jnp.exp(b[rows(16 * j, 16 * j + 16)] - ref)
                          for j in (2, 3)]
    k2b = [k[rows(16 * j, 16 * j + 16)] * jnp.exp(ref - b[rows(16 * j, 16 * j + 16)])
           for j in (0, 1)] + [None, None]
    q2, k2 = assemble(q2b), assemble(k2b)
    b_last = brow(C - 1)
    q3 = (q * jnp.exp(b)).astype(BF16)
    k3 = (k * jnp.exp(b_last - b)).astype(BF16)
    dec = jnp.exp(b_last)

    m0 = ((pr >> 4) == (pc >> 4)) & (pc <= pr)
    m1 = ((pr >> 5) == (pc >> 5)) & (((pr >> 4) & 1) == 1) & (((pc >> 4) & 1) == 0)

    heads = [slice(h * HG_D, (h + 1) * HG_D) for h in range(HG_HEADS)]
    states = [s_scr[h] for h in range(HG_HEADS)]
    stage1 = []
    for h, sl in enumerate(heads):
        a0 = lax.dot_general(q0[:, sl], k0[:, sl], NT_DIMS, preferred_element_type=F32)
        a1 = lax.dot_general(q1[:, sl], k1[:, sl], NT_DIMS, preferred_element_type=F32)
        a2 = lax.dot_general(q2[:, sl], k2[:, sl], NT_DIMS, preferred_element_type=F32)
        inter = lax.dot_general(q3[:, sl], states[h].astype(BF16), NT_DIMS, preferred_element_type=F32)
        upd = lax.dot_general(vb[:, sl], k3[:, sl], TN_DIMS, preferred_element_type=F32)
        stage1.append((a0, a1, a2, inter, upd))
    outs = []
    if readout:
        o_back, out_gate = chunk_rows(ob_ref), chunk_rows(sg_ref)
    for h, sl in enumerate(heads):
        a0, a1, a2, inter, upd = stage1[h]
        scores = jnp.where(m0, a0, jnp.where(m1, a1, a2))
        o_h = jnp.dot(scores.astype(BF16), vb[:, sl], preferred_element_type=F32) + inter
        if readout:
            o_t = o_h + o_back[:, sl]
            o_h = o_t * _rms_scale(o_t) * hgg_ref[:, sl] * out_gate[:, sl]
        outs.append(o_h.astype(o_ref.dtype))
    for h, sl in enumerate(heads):
        s_scr[h] = states[h] * dec[:, sl] + stage1[h][4]
    o_ref[...] = jnp.concatenate(outs, axis=1).reshape(o_ref.shape)

    @pl.when(i == pl.num_programs(1) - 1)
    def _():
        sfin_ref[...] = s_scr[...]


def _hgrn_call(p_all, s0, lb_dir_col, *, latent, rev, o_back=None, hgg=None, out_buf=None):
    readout = o_back is not None
    aliased = out_buf is not None
    n_chunks = (SEQ if latent else CTX_LEN) // CHUNK
    ctx_row0 = T_LAT // CHUNK
    out_dtype = BF16 if readout else F32

    def chunk(i):
        return (n_chunks - 1 - i) if rev else i

    if latent:
        def view(a):
            return a.reshape(T_ALL // BAND, BAND, a.shape[-1])

        def spec(col):
            return pl.BlockSpec((BANDS, SLAB, HG_WIDTH), lambda b, i: (b, chunk(i), col))
    else:
        def view(a):
            return a

        def spec(col):
            return pl.BlockSpec((CHUNK, HG_WIDTH), lambda b, i: (ctx_row0 + b * n_chunks + chunk(i), col))

    p_view = view(p_all)
    o_shape = p_view.shape[:-1] + (HG_WIDTH,)
    state_spec = pl.BlockSpec((None, HG_HEADS, HG_D, HG_D), lambda b, i: (b, 0, 0, 0))
    in_specs = [spec(COL_Q), spec(lb_dir_col), spec(COL_V), state_spec]
    args = [p_view, p_view, p_view, s0]
    if readout:
        in_specs += [spec(0), spec(COL_G), pl.BlockSpec((1, HG_WIDTH), lambda b, i: (0, 0))]
        args += [view(o_back), p_view, hgg]
    aliases = {}
    if aliased:
        in_specs.append(pl.BlockSpec(memory_space=pl.ANY))
        args.append(view(out_buf))
        aliases = {len(args) - 1: 0}

    o, s_fin = pl.pallas_call(
        functools.partial(_hgrn_kernel, rev=rev, readout=readout, aliased=aliased),
        grid=(BATCH, n_chunks),
        in_specs=in_specs,
        out_specs=[spec(0), state_spec],
        out_shape=[jax.ShapeDtypeStruct(o_shape, out_dtype),
                   jax.ShapeDtypeStruct((BATCH, HG_HEADS, HG_D, HG_D), F32)],
        scratch_shapes=[pltpu.VMEM((HG_HEADS, HG_D, HG_D), F32)],
        input_output_aliases=aliases,
        compiler_params=_params("parallel", "arbitrary"),
        name="hgrn2_%s_%s" % ("latent" if latent else "context", "bwd" if rev else "fwd"),
    )(*args)
    return o.reshape(T_ALL, HG_WIDTH), s_fin


def _hgrn_mixer(p_all, hgg):
    zero_s = jnp.zeros((BATCH, HG_HEADS, HG_D, HG_D), F32)
    o_b, s_b = _hgrn_call(p_all, zero_s, COL_FB, latent=False, rev=True)
    o_b, _ = _hgrn_call(p_all, s_b, COL_FB, latent=True, rev=True, out_buf=o_b)
    hg, s_f = _hgrn_call(p_all, zero_s, COL_FF, latent=False, rev=False, o_back=o_b, hgg=hgg)
    hg, _ = _hgrn_call(p_all, s_f, COL_FF, latent=True, rev=False, o_back=o_b, hgg=hgg, out_buf=hg)
    return hg


def _rg_gates(xc, wa_ref, ba_ref, wx_ref, bx_ref, lam_ref):
    xcb = xc.astype(BF16)
    r_parts, i_parts = [], []
    for n in range(RG_BLOCKS):
        sl = slice(n * RG_BW, (n + 1) * RG_BW)
        r_parts.append(jnp.dot(xcb[:, sl], wa_ref[n], preferred_element_type=F32))
        i_parts.append(jnp.dot(xcb[:, sl], wx_ref[n], preferred_element_type=F32))
    r = _sigmoid(jnp.concatenate(r_parts, axis=1) + ba_ref[...])
    ig = _sigmoid(jnp.concatenate(i_parts, axis=1) + bx_ref[...])
    nl = -lam_ref[...]
    softplus = jnp.maximum(nl, 0.0) + jnp.log1p(jnp.exp(-jnp.abs(nl)))
    log_a = (-RG_C) * r * softplus
    a = jnp.exp(log_a)
    return a, jnp.sqrt(-jnp.tanh(log_a) * (1.0 + a * a)) * ig * xc


def _rg_ctx_kernel(*refs, rev, final):
    x_ref, cw_ref, cb_ref, wa_ref, ba_ref, wx_ref, bx_ref, lam_ref = refs[:8]
    if final:
        hb_ref, gate_ref, o_ref, hfin_ref = refs[8:]
    else:
        o_ref, hfin_ref = refs[8:]
    R = CTX_LEN
    N = R + 2 * SUBLANES
    x = x_ref[...]
    pad = jnp.zeros((SUBLANES, RG_WIDTH), F32)
    ext = jnp.concatenate([pad, x, pad], axis=0)
    lo, hi_ = SUBLANES, SUBLANES + R
    xc = (cw_ref[0:1, :] * pltpu.roll(ext, 2, 0)[lo:hi_]
          + cw_ref[1:2, :] * pltpu.roll(ext, 1, 0)[lo:hi_]
          + cw_ref[2:3, :] * x
          + cw_ref[3:4, :] * pltpu.roll(ext, N - 1, 0)[lo:hi_]
          + cb_ref[...])
    a, bt = _rg_gates(xc, wa_ref, ba_ref, wx_ref, bx_ref, lam_ref)

    row = lax.broadcasted_iota(jnp.int32, (R, RG_WIDTH), 0)
    step = 1
    while step < R:
        if rev:
            keep = row < R - step
            shift = R - step
        else:
            keep = row >= step
            shift = step
        a_sh = jnp.where(keep, pltpu.roll(a, shift, 0), 1.0)
        b_sh = jnp.where(keep, pltpu.roll(bt, shift, 0), 0.0)
        bt = bt + a * b_sh
        a = a * a_sh
        step *= 2
    h = bt
    end = 0 if rev else R - 1
    hfin_ref[...] = h[end:end + 1, :]
    if final:
        o_ref[...] = ((h + hb_ref[...]) * gate_ref[...]).astype(o_ref.dtype)
    else:
        o_ref[...] = h


def _rg_lat_kernel(*refs, rev, final):
    xp_ref, x_ref, xn_ref, cw_ref, cb_ref, wa_ref, ba_ref, wx_ref, bx_ref, lam_ref, h0_ref = refs[:11]
    if final:
        hb_ref, gate_ref, _alias, o_ref, ext, a_scr, h_scr, carry = refs[11:]
    else:
        _alias, o_ref, ext, a_scr, h_scr, carry = refs[11:]
    S = SLAB
    j = pl.program_id(1)
    band = (BANDS - 1 - j) if rev else j

    @pl.when(j == 0)
    def _():
        carry[...] = h0_ref[...]

    row = lax.broadcasted_iota(jnp.int32, (S, RG_WIDTH), 0)
    has_prev = (band > 0).astype(F32)
    has_next = (band < BANDS - 1).astype(F32)

    def shift_down(slab, top):
        return jnp.where(row == 0, top, pltpu.roll(slab, 1, 0))

    def shift_up(slab, bottom):
        return jnp.where(row == S - 1, bottom, pltpu.roll(slab, S - 1, 0))

    ext[0:S, :] = shift_down(x_ref[BAND - 2 * S:BAND - S, :], xp_ref[S - 1:S, :] * has_prev)
    ext[S:2 * S, :] = shift_down(x_ref[BAND - S:BAND, :], xp_ref[2 * S - 1:2 * S, :] * has_prev)
    ext[2 * S:2 * S + BAND, :] = x_ref[...]
    ext[2 * S + BAND:3 * S + BAND, :] = shift_up(x_ref[0:S, :], xn_ref[0:1, :] * has_next)

    CH = 256
    for s in range(0, BAND, CH):
        xc = (cw_ref[0:1, :] * ext[s:s + CH, :]
              + cw_ref[1:2, :] * ext[s + S:s + S + CH, :]
              + cw_ref[2:3, :] * ext[s + 2 * S:s + 2 * S + CH, :]
              + cw_ref[3:4, :] * ext[s + 3 * S:s + 3 * S + CH, :]
              + cb_ref[...])
        a, bt = _rg_gates(xc, wa_ref, ba_ref, wx_ref, bx_ref, lam_ref)
        a_scr[s:s + CH, :] = a
        h_scr[s:s + CH, :] = bt

    def body(t, hc):
        h, ac = hc
        c = (GRID_W - 1 - t) if rev else t
        sl = pl.ds(pl.multiple_of(c * S, S), S)
        a_c = a_scr[sl, :]
        h = a_c * h + h_scr[sl, :]
        ac = a_c * ac
        h_scr[sl, :] = h
        a_scr[sl, :] = ac
        return h, ac

    h_end, a_end = lax.fori_loop(0, GRID_W, body,
                                 (jnp.zeros((S, RG_WIDTH), F32), jnp.ones((S, RG_WIDTH), F32)),
                                 unroll=4)
    cin = jnp.zeros((S, RG_WIDTH), F32)
    c_run = carry[...]
    order = range(S - 1, -1, -1) if rev else range(S)
    for r in order:
        cin = jnp.where(row == r, c_run, cin)
        c_run = h_end[r:r + 1, :] + a_end[r:r + 1, :] * c_run
    carry[...] = c_run
    h = (h_scr[...].reshape(GRID_W, S, RG_WIDTH)
         + a_scr[...].reshape(GRID_W, S, RG_WIDTH) * cin[None]).reshape(BAND, RG_WIDTH)
    if final:
        o_ref[...] = ((h + hb_ref[...]) * gate_ref[...]).astype(o_ref.dtype)
    else:
        o_ref[...] = h


def _rg_call(p_all, w, *, rev, h_back=None):
    cw, cb, wa, ba, wx, bx, lam = w
    final = h_back is not None
    out_dtype = BF16 if final else F32

    def const(shape):
        return pl.BlockSpec(shape, lambda *_: (0,) * len(shape))

    w_specs = [const((4, RG_WIDTH)), const((1, RG_WIDTH)),
               const((RG_BLOCKS, RG_BW, RG_BW)), const((1, RG_WIDTH)),
               const((RG_BLOCKS, RG_BW, RG_BW)), const((1, RG_WIDTH)), const((1, RG_WIDTH))]
    w_args = [cw, cb, wa, ba, wx, bx, lam]
    carry_spec = pl.BlockSpec((None, 1, RG_WIDTH), lambda b, *_: (b, 0, 0))
    carry_shape = jax.ShapeDtypeStruct((BATCH, 1, RG_WIDTH), F32)

    def cmain(col):
        return pl.BlockSpec((CTX_LEN, RG_WIDTH), lambda b: (T_LAT // CTX_LEN + b, col))

    in_specs = [cmain(COL_RX)] + w_specs
    args = [p_all] + w_args
    if final:
        in_specs += [cmain(0), cmain(COL_RGATE)]
        args += [h_back, p_all]
    out_c, h_ctx = pl.pallas_call(
        functools.partial(_rg_ctx_kernel, rev=rev, final=final),
        grid=(BATCH,),
        in_specs=in_specs,
        out_specs=[cmain(0), carry_spec],
        out_shape=[jax.ShapeDtypeStruct((T_ALL, RG_WIDTH), out_dtype), carry_shape],
        compiler_params=_params("parallel"),
        name="rglru_context_%s" % ("bwd" if rev else "fwd"),
    )(*args)

    def band_idx(b, j):
        return b * BANDS + ((BANDS - 1 - j) if rev else j)

    def lmain(col):
        return pl.BlockSpec((BAND, RG_WIDTH), lambda b, j: (band_idx(b, j), col))

    per = BAND // (2 * SLAB)
    prev_spec = pl.BlockSpec((2 * SLAB, RG_WIDTH),
                             lambda b, j: (jnp.maximum(band_idx(b, j) * per - 1, 0), COL_RX))
    next_spec = pl.BlockSpec((SLAB, RG_WIDTH),
                             lambda b, j: ((band_idx(b, j) + 1) * (BAND // SLAB), COL_RX))
    in_specs = [prev_spec, lmain(COL_RX), next_spec] + w_specs + [carry_spec]
    args = [p_all, p_all, p_all] + w_args + [h_ctx]
    if final:
        in_specs += [lmain(0), lmain(COL_RGATE)]
        args += [h_back, p_all]
    in_specs.append(pl.BlockSpec(memory_space=pl.ANY))
    args.append(out_c)
    return pl.pallas_call(
        functools.partial(_rg_lat_kernel, rev=rev, final=final),
        grid=(BATCH, BANDS),
        in_specs=in_specs,
        out_specs=lmain(0),
        out_shape=jax.ShapeDtypeStruct((T_ALL, RG_WIDTH), out_dtype),
        scratch_shapes=[pltpu.VMEM((BAND + 3 * SLAB, RG_WIDTH), F32),
                        pltpu.VMEM((BAND, RG_WIDTH), F32),
                        pltpu.VMEM((BAND, RG_WIDTH), F32),
                        pltpu.VMEM((1, RG_WIDTH), F32)],
        input_output_aliases={len(args) - 1: 0},
        compiler_params=_params("parallel", "arbitrary"),
        name="rglru_latent_%s" % ("bwd" if rev else "fwd"),
    )(*args)


def _rg_mixer(p_all, cw, cb, wa, ba, wx, bx, lam):
    row = lambda t: t.reshape(1, RG_WIDTH)

    def weights(d):
        return (cw, row(cb), wa[d].astype(BF16), row(ba[d]), wx[d].astype(BF16), row(bx[d]), row(lam[d]))

    h_b = _rg_call(p_all, weights(1), rev=True)
    return _rg_call(p_all, weights(0), rev=False, h_back=h_b)


def _outproj_kernel(hg_ref, rg_ref, w_ref, x_ref, mod_ref, g_ref, o_ref):
    mix = (jnp.dot(hg_ref[...], w_ref[0:HG_WIDTH, :], preferred_element_type=F32)
           + jnp.dot(rg_ref[...], w_ref[HG_WIDTH:, :], preferred_element_type=F32))
    y = mix * _rms_scale(mix) * g_ref[1:2, :]
    o_ref[...] = x_ref[...] + mod_ref[2:3, :] * y


def _outproj(hg, rg, w_out_l, x_all, mod_l, norm_g_l, n_rows):
    tm = 512
    return pl.pallas_call(
        _outproj_kernel,
        grid=(n_rows // tm,),
        in_specs=[
            pl.BlockSpec((tm, HG_WIDTH), lambda i: (i, 0)),
            pl.BlockSpec((tm, RG_WIDTH), lambda i: (i, 0)),
            pl.BlockSpec((D_MODEL, D_MODEL), lambda i: (0, 0)),
            pl.BlockSpec((tm, D_MODEL), lambda i: (i, 0)),
            pl.BlockSpec((None, 6, D_MODEL), lambda i: (_mod_row(i, tm), 0, 0)),
            pl.BlockSpec((4, D_MODEL), lambda i: (0, 0)),
        ],
        out_specs=pl.BlockSpec((tm, D_MODEL), lambda i: (i, 0)),
        out_shape=jax.ShapeDtypeStruct((T_ALL, D_MODEL), F32),
        input_output_aliases={3: 0},
        compiler_params=_params("parallel"),
        name="out_projection",
    )(hg, rg, w_out_l, x_all, mod_l, norm_g_l)


FFN_TM = BAND
FFN_HALO = SLAB


def _ffn_up_kernel(*refs, banded, tile0):
    xp_ref, xm_ref, xn_ref, mod_ref, g_ref, wa_ref, wv_ref, cwa_ref, cwv_ref, cba_ref, cbv_ref = refs[:11]
    o_ref, h_ref = refs[-2:]
    i = pl.program_id(0) + tile0
    n = pl.program_id(1)
    tm, H, S = FFN_TM, FFN_HALO, SLAB
    N = tm + 2 * H

    @pl.when(n == 0)
    def _():
        def norm_mod(x):
            y = x * _rms_scale(x) * g_ref[2:3, :]
            return (y * (1.0 + mod_ref[4:5, :]) + mod_ref[3:4, :]).astype(BF16)
        h_ref[0:H, :] = norm_mod(xp_ref[...])
        h_ref[H:H + tm, :] = norm_mod(xm_ref[...])
        h_ref[H + tm:N, :] = norm_mod(xn_ref[...])

    if banded:
        band = i % BANDS
        has_prev = (band > 0).astype(F32)
        has_next = (band < BANDS - 1).astype(F32)
        row = lax.broadcasted_iota(jnp.int32, (S, o_ref.shape[1]), 0)

        def conv(u, cw, cb):
            mid = u[H:H + tm]
            first = jnp.where(row == 0, u[H - 1:H] * has_prev, pltpu.roll(mid[tm - S:tm], 1, 0))
            last = jnp.where(row == S - 1, u[H + tm:H + tm + 1] * has_next, pltpu.roll(mid[0:S], S - 1, 0))
            prev = jnp.concatenate([first, mid[0:tm - S]], axis=0)
            nxt = jnp.concatenate([mid[S:tm], last], axis=0)
            return cw[0:1] * prev + cw[1:2] * mid + cw[2:3] * nxt + cb
    else:
        pos = lax.broadcasted_iota(jnp.int32, (tm, 1), 0) % CTX_LEN
        has_prev = (pos != 0).astype(F32)
        has_next = (pos != CTX_LEN - 1).astype(F32)

        def conv(u, cw, cb):
            return (cw[0:1] * (pltpu.roll(u, 1, 0)[H:H + tm] * has_prev)
                    + cw[1:2] * u[H:H + tm]
                    + cw[2:3] * (pltpu.roll(u, N - 1, 0)[H:H + tm] * has_next)
                    + cb)

    hh = h_ref[...]
    ua = conv(jnp.dot(hh, wa_ref[...], preferred_element_type=F32), cwa_ref[...], cba_ref[...])
    uv = conv(jnp.dot(hh, wv_ref[...], preferred_element_type=F32), cwv_ref[...], cbv_ref[...])
    o_ref[...] = (_gelu_tanh(ua) * uv).astype(o_ref.dtype)


def _ffn_up_call(x_all, mod_l, norm_g_l, w_up_l, cw_l, cb2, *, banded, tile0, n_tiles, n_rows, out_buf):
    tm, tn = FFN_TM, 512
    nn = D_FF // tn
    per = tm // FFN_HALO
    last = T_ALL // FFN_HALO - 1
    in_specs = [
        pl.BlockSpec((FFN_HALO, D_MODEL), lambda i, n: (jnp.maximum((i + tile0) * per - 1, 0), 0)),
        pl.BlockSpec((tm, D_MODEL), lambda i, n: (i + tile0, 0)),
        pl.BlockSpec((FFN_HALO, D_MODEL), lambda i, n: (jnp.minimum((i + tile0 + 1) * per, last), 0)),
        pl.BlockSpec((None, 6, D_MODEL), lambda i, n: (_mod_row(i + tile0, tm), 0, 0)),
        pl.BlockSpec((4, D_MODEL), lambda i, n: (0, 0)),
        pl.BlockSpec((D_MODEL, tn), lambda i, n: (0, n)),
        pl.BlockSpec((D_MODEL, tn), lambda i, n: (0, nn + n)),
        pl.BlockSpec((3, tn), lambda i, n: (0, n)),
        pl.BlockSpec((3, tn), lambda i, n: (0, nn + n)),
        pl.BlockSpec((1, tn), lambda i, n: (0, n)),
        pl.BlockSpec((1, tn), lambda i, n: (0, nn + n)),
    ]
    args = [x_all, x_all, x_all, mod_l, norm_g_l, w_up_l, w_up_l, cw_l, cw_l, cb2, cb2]
    aliases = {}
    if out_buf is not None:
        in_specs.append(pl.BlockSpec(memory_space=pl.ANY))
        args.append(out_buf)
        aliases = {len(args) - 1: 0}
    return pl.pallas_call(
        functools.partial(_ffn_up_kernel, banded=banded, tile0=tile0),
        grid=(n_tiles, nn),
        in_specs=in_specs,
        out_specs=pl.BlockSpec((tm, tn), lambda i, n: (i + tile0, n)),
        out_shape=jax.ShapeDtypeStruct((n_rows, D_FF), BF16),
        scratch_shapes=[pltpu.VMEM((tm + 2 * FFN_HALO, D_MODEL), BF16)],
        input_output_aliases=aliases,
        compiler_params=_params("parallel", "arbitrary"),
        name="ffn_up_%s" % ("latent" if banded else "context"),
    )(*args)


def _ffn_up(x_all, mod_l, norm_g_l, w_up_l, cw_l, cb_l, n_rows):
    cb2 = cb_l.reshape(1, 2 * D_FF)
    act = _ffn_up_call(x_all, mod_l, norm_g_l, w_up_l, cw_l, cb2, banded=True, tile0=0,
                       n_tiles=T_LAT // FFN_TM, n_rows=n_rows, out_buf=None)
    if n_rows > T_LAT:
        act = _ffn_up_call(x_all, mod_l, norm_g_l, w_up_l, cw_l, cb2, banded=False,
                           tile0=T_LAT // FFN_TM, n_tiles=T_CTX // FFN_TM, n_rows=n_rows, out_buf=act)
    return act


def _ffn_down_kernel(a_ref, w_ref, x_ref, mod_ref, g_ref, o_ref):
    f = jnp.dot(a_ref[...], w_ref[...], preferred_element_type=F32)
    y = f * _rms_scale(f) * g_ref[3:4, :]
    o_ref[...] = x_ref[...] + mod_ref[5:6, :] * y


def _ffn_down(act, w_down_l, x_all, mod_l, norm_g_l, n_rows, in_place):
    tm = 256
    return pl.pallas_call(
        _ffn_down_kernel,
        grid=(n_rows // tm,),
        in_specs=[
            pl.BlockSpec((tm, D_FF), lambda i: (i, 0)),
            pl.BlockSpec((D_FF, D_MODEL), lambda i: (0, 0), pipeline_mode=pl.Buffered(1)),
            pl.BlockSpec((tm, D_MODEL), lambda i: (i, 0)),
            pl.BlockSpec((None, 6, D_MODEL), lambda i: (_mod_row(i, tm), 0, 0)),
            pl.BlockSpec((4, D_MODEL), lambda i: (0, 0)),
        ],
        out_specs=pl.BlockSpec((tm, D_MODEL), lambda i: (i, 0)),
        out_shape=jax.ShapeDtypeStruct((T_ALL if in_place else n_rows, D_MODEL), F32),
        input_output_aliases={2: 0} if in_place else {},
        compiler_params=_params("parallel"),
        name="ffn_down",
    )(act, w_down_l, x_all, mod_l, norm_g_l)


def kernel(x, c, ctx, c_ctx, w_mod, b_mod, norm_g, w_in, hg_lower_bounds, hg_norm_g, rg_conv_w,
           rg_conv_b, rg_wa, rg_ba, rg_wx, rg_bx, rg_lambda, w_out, ffn_w_up, ffn_conv_w,
           ffn_conv_b, ffn_w_down):
    x_all = jnp.concatenate([_to_banded(x), ctx.reshape(T_CTX, D_MODEL)], axis=0)
    cc = jnp.concatenate([c, c_ctx[None, :], jnp.zeros((MOD_ROWS - BATCH - 1, D_MODEL), F32)], axis=0)
    mods = _modulation(cc, w_mod, b_mod)
    lb_all = _lower_bounds(hg_lower_bounds)

    for l in range(DEPTH):
        ctx_out = l < DEPTH - 1
        n_rows = T_ALL if ctx_out else T_LAT
        p_all = _inproj(x_all, mods[l], norm_g[l], lb_all[l], w_in[l].astype(BF16))
        hg = _hgrn_mixer(p_all, hg_norm_g[l].reshape(1, HG_WIDTH))
        rg = _rg_mixer(p_all, rg_conv_w[l], rg_conv_b[l], rg_wa[l], rg_ba[l], rg_wx[l], rg_bx[l],
                       rg_lambda[l])
        x_all = _outproj(hg, rg, w_out[l].astype(BF16), x_all, mods[l], norm_g[l], n_rows)
        act = _ffn_up(x_all, mods[l], norm_g[l], ffn_w_up[l].astype(BF16), ffn_conv_w[l],
                      ffn_conv_b[l], n_rows)
        x_all = _ffn_down(act, ffn_w_down[l].astype(BF16), x_all, mods[l], norm_g[l], n_rows,
                          in_place=ctx_out)
    return _from_banded(x_all)
```

```python
import functools
import math

import jax
import jax.numpy as jnp
from jax import lax
from jax.experimental import pallas as pl
from jax.experimental.pallas import tpu as pltpu

F32 = jnp.float32
BF16 = jnp.bfloat16

D_MODEL = 2048
BATCH = 4
SEQ = 4096
DEPTH = 4
GRID_W = 64
GRID_ROWS = SEQ // GRID_W
CTX_LEN = 256
HG_WIDTH = 1024
HG_HEADS = 8
HG_D = HG_WIDTH // HG_HEADS
CHUNK = 64
RG_WIDTH = 1024
RG_BLOCKS = 8
RG_BW = RG_WIDTH // RG_BLOCKS
RG_C = 8.0
D_FF = 5632
N_PROJ = 5 * HG_WIDTH + 2 * RG_WIDTH
EPS = 1e-6

T_LAT = BATCH * SEQ
T_CTX = BATCH * CTX_LEN
T_ALL = T_LAT + T_CTX
MOD_ROWS = 8
SUBLANES = 8
SLAB = 16
BAND = SLAB * GRID_W
BANDS = SEQ // BAND
EXP_CLAMP = 80.0
HG_SEQS = 2

VMEM_LIMIT = 56 * 1024 * 1024

COL_Q, COL_FF, COL_FB, COL_V, COL_G, COL_RX, COL_RGATE = range(7)

NT_DIMS = (((1,), (1,)), ((), ()))
TN_DIMS = (((0,), (0,)), ((), ()))


def _sigmoid(x):
    return 0.5 + 0.5 * jnp.tanh(0.5 * x)


def _silu(x):
    return x * _sigmoid(x)


def _gelu_tanh(x):
    c = math.sqrt(2.0 / math.pi)
    return x * (0.5 * (1.0 + jnp.tanh(c * (x + 0.044715 * (x * x * x)))))


def _rms_scale(x):
    return lax.rsqrt(jnp.mean(x * x, axis=-1, keepdims=True) + EPS)


def _params(*sem):
    return pltpu.CompilerParams(dimension_semantics=sem, vmem_limit_bytes=VMEM_LIMIT)


def _mod_row(i, tm):
    return jnp.minimum(i // (SEQ // tm), BATCH)


def _to_banded(x):
    t = x.reshape(BATCH, BANDS, SLAB, GRID_W, D_MODEL)
    return t.transpose(0, 1, 3, 2, 4).reshape(T_LAT, D_MODEL)


def _from_banded(y):
    t = y.reshape(BATCH, BANDS, GRID_W, SLAB, D_MODEL)
    return t.transpose(0, 1, 3, 2, 4).reshape(BATCH, SEQ, D_MODEL)


def _lb_kernel(x_ref, o_ref):
    x = x_ref[...]
    m = jnp.max(x, axis=0, keepdims=True)
    e = jnp.exp(x - m)
    sm = e / jnp.sum(e, axis=0, keepdims=True)
    acc = jnp.zeros_like(sm[0:1])
    o_ref[0:1, :] = acc
    for l in range(1, DEPTH):
        acc = acc + sm[l:l + 1]
        o_ref[l:l + 1, :] = acc


def _lower_bounds(hg_lower_bounds):
    x = hg_lower_bounds.reshape(DEPTH, 2 * HG_WIDTH)
    out = pl.pallas_call(
        _lb_kernel,
        out_shape=jax.ShapeDtypeStruct((DEPTH, 2 * HG_WIDTH), F32),
        name="hg_lower_bounds",
    )(x)
    return out.reshape(DEPTH, 2, HG_WIDTH)


def _mod_kernel(cc_ref, w_ref, b_ref, o_ref):
    s = _silu(cc_ref[...])
    o_ref[...] = jnp.dot(s.astype(BF16), w_ref[...].astype(BF16),
                         preferred_element_type=F32) + b_ref[...]


def _modulation(cc, w_mod, b_mod):
    tn = 1024
    n_out = 6 * D_MODEL
    out = pl.pallas_call(
        _mod_kernel,
        grid=(DEPTH, n_out // tn),
        in_specs=[
            pl.BlockSpec((MOD_ROWS, D_MODEL), lambda l, n: (0, 0)),
            pl.BlockSpec((None, D_MODEL, tn), lambda l, n: (l, 0, n)),
            pl.BlockSpec((None, 1, tn), lambda l, n: (l, 0, n)),
        ],
        out_specs=pl.BlockSpec((None, MOD_ROWS, tn), lambda l, n: (l, 0, n)),
        out_shape=jax.ShapeDtypeStruct((DEPTH, MOD_ROWS, n_out), F32),
        compiler_params=_params("parallel", "parallel"),
        name="adaln_modulation",
    )(cc, w_mod, b_mod.reshape(DEPTH, 1, n_out))
    return out.reshape(DEPTH, MOD_ROWS, 6, D_MODEL)


def _inproj_kernel(x_ref, mod_ref, g_ref, lb_ref, w_ref, o_ref):
    x = x_ref[...]
    y = x * _rms_scale(x) * g_ref[0:1, :]
    h = (y * (1.0 + mod_ref[1:2, :]) + mod_ref[0:1, :]).astype(BF16)

    def gate(lb):
        return lambda z: lb + (1.0 - lb) * _sigmoid(z)

    acts = {COL_Q: _silu, COL_FF: gate(lb_ref[0:1, :]), COL_FB: gate(lb_ref[1:2, :]), COL_V: None,
            COL_G: _silu, COL_RX: None, COL_RGATE: _gelu_tanh}
    for col, act in acts.items():
        cs = slice(col * HG_WIDTH, (col + 1) * HG_WIDTH)
        z = jnp.dot(h, w_ref[:, cs], preferred_element_type=F32)
        o_ref[:, cs] = z if act is None else act(z)


def _inproj(x_all, mod_l, norm_g_l, lb_l, w_in, l):
    tm = 256
    return pl.pallas_call(
        _inproj_kernel,
        grid=(T_ALL // tm,),
        in_specs=[
            pl.BlockSpec((tm, D_MODEL), lambda i: (i, 0)),
            pl.BlockSpec((None, 6, D_MODEL), lambda i: (_mod_row(i, tm), 0, 0)),
            pl.BlockSpec((4, D_MODEL), lambda i: (0, 0)),
            pl.BlockSpec((2, HG_WIDTH), lambda i: (0, 0)),
            pl.BlockSpec((None, D_MODEL, N_PROJ), lambda i: (l, 0, 0), pipeline_mode=pl.Buffered(1)),
        ],
        out_specs=pl.BlockSpec((tm, N_PROJ), lambda i: (i, 0)),
        out_shape=jax.ShapeDtypeStruct((T_ALL, N_PROJ), F32),
        compiler_params=_params("parallel"),
        name="in_projection",
    )(x_all, mod_l, norm_g_l, lb_l, w_in)


def _hgrn_chunk(q, f, v, states, rev):
    C = CHUNK
    NB = C // 16
    vb = v.astype(BF16)
    k = 1.0 - f
    g = jnp.log(f)

    def scan_pos(idx):
        return (C - 1 - idx) if rev else idx

    pr = scan_pos(lax.broadcasted_iota(jnp.int32, (C, C), 0))
    pc = scan_pos(lax.broadcasted_iota(jnp.int32, (C, C), 1))
    tri = (pc <= pr).astype(BF16)
    g_hi = g.astype(BF16)
    r1 = g - g_hi.astype(F32)
    g_mid = r1.astype(BF16)
    g_lo = (r1 - g_mid.astype(F32)).astype(BF16)
    b = (jnp.dot(tri, g_hi, preferred_element_type=F32)
         + jnp.dot(tri, g_mid, preferred_element_type=F32)
         + jnp.dot(tri, g_lo, preferred_element_type=F32))

    def rows(p0, p1):
        return slice(C - p1, C - p0) if rev else slice(p0, p1)

    def brow(sp):
        r = scan_pos(sp)
        return b[r:r + 1, :]

    def assemble(blocks):
        zero = jnp.zeros((16, HG_WIDTH), F32)
        blocks = [zero if blk is None else blk for blk in blocks]
        if rev:
            blocks = blocks[::-1]
        return jnp.concatenate(blocks, axis=0).astype(BF16)

    q0b, k0b = [], []
    for j in range(NB):
        R = rows(16 * j, 16 * j + 16)
        d = jnp.clip(b[R] - 0.5 * (brow(16 * j) + brow(16 * j + 15)), -EXP_CLAMP, EXP_CLAMP)
        q0b.append(q[R] * jnp.exp(d))
        k0b.append(k[R] * jnp.exp(-d))
    q0, k0 = assemble(q0b), assemble(k0b)
    q1b, k1b = [], []
    for m in range(NB // 2):
        ref = brow(32 * m + 15)
        Re, Ro = rows(32 * m, 32 * m + 16), rows(32 * m + 16, 32 * m + 32)
        k1b += [k[Re] * jnp.exp(ref - b[Re]), None]
        q1b += [None, q[Ro] * jnp.exp(b[Ro] - ref)]
    q1, k1 = assemble(q1b), assemble(k1b)
    ref = brow(31)
    q2b = [None, None] + [q[rows(16 * j, 16 * j + 16)] * jnp.exp(b[rows(16 * j, 16 * j + 16)] - ref)
                          for j in (2, 3)]
    k2b = [k[rows(16 * j, 16 * j + 16)] * jnp.exp(ref - b[rows(16 * j, 16 * j + 16)])
           for j in (0, 1)] + [None, None]
    q2, k2 = assemble(q2b), assemble(k2b)
    b_last = brow(C - 1)
    q3 = (q * jnp.exp(b)).astype(BF16)
    k3 = (k * jnp.exp(b_last - b)).astype(BF16)
    dec = jnp.exp(b_last)

    m0 = ((pr >> 4) == (pc >> 4)) & (pc <= pr)
    m1 = ((pr >> 5) == (pc >> 5)) & (((pr >> 4) & 1) == 1) & (((pc >> 4) & 1) == 0)

    heads = [slice(h * HG_D, (h + 1) * HG_D) for h in range(HG_HEADS)]
    stage1 = []
    for h, sl in enumerate(heads):
        a0 = lax.dot_general(q0[:, sl], k0[:, sl], NT_DIMS, preferred_element_type=F32)
        a1 = lax.dot_general(q1[:, sl], k1[:, sl], NT_DIMS, preferred_element_type=F32)
        a2 = lax.dot_general(q2[:, sl], k2[:, sl], NT_DIMS, preferred_element_type=F32)
        inter = lax.dot_general(q3[:, sl], states[h].astype(BF16), NT_DIMS, preferred_element_type=F32)
        upd = lax.dot_general(vb[:, sl], k3[:, sl], TN_DIMS, preferred_element_type=F32)
        stage1.append((a0, a1, a2, inter, upd))
    outs, new_states = [], []
    for h, sl in enumerate(heads):
        a0, a1, a2, inter, upd = stage1[h]
        scores = jnp.where(m0, a0, jnp.where(m1, a1, a2))
        outs.append(jnp.dot(scores.astype(BF16), vb[:, sl], preferred_element_type=F32) + inter)
        new_states.append(states[h] * dec[:, sl] + upd)
    return jnp.concatenate(outs, axis=1), new_states


def _hgrn_kernel(qf_ref, ff_ref, vf_ref, qb_ref, fb_ref, vb_ref, s0_ref, *rest):
    of_ref, ob_ref, sfin_ref, s_scr = rest[-4:]
    i = pl.program_id(1)

    @pl.when(i == 0)
    def _():
        s_scr[...] = s0_ref[...]

    def chunk_rows(ref, s):
        return ref[...].reshape(HG_SEQS, CHUNK, HG_WIDTH)[s]

    results = {}
    for s in range(HG_SEQS):
        for d, (q_ref, f_ref, v_ref) in enumerate(((qf_ref, ff_ref, vf_ref), (qb_ref, fb_ref, vb_ref))):
            states = [s_scr[s, d, h] for h in range(HG_HEADS)]
            results[s, d] = _hgrn_chunk(chunk_rows(q_ref, s), chunk_rows(f_ref, s), chunk_rows(v_ref, s),
                                        states, rev=(d == 1))
    for d, o_ref in enumerate((of_ref, ob_ref)):
        for s in range(HG_SEQS):
            for h in range(HG_HEADS):
                s_scr[s, d, h] = results[s, d][1][h]
        o_ref[...] = jnp.stack([results[s, d][0] for s in range(HG_SEQS)]).reshape(o_ref.shape)

    @pl.when(i == pl.num_programs(1) - 1)
    def _():
        sfin_ref[...] = s_scr[...]


def _hgrn_call(p_all, s0, *, latent, out_bufs=None):
    n_chunks = (SEQ if latent else CTX_LEN) // CHUNK
    G = HG_SEQS

    if latent:
        def view(a):
            return a.reshape(T_ALL // BAND, BAND, a.shape[-1])

        def spec(col, rev):
            return pl.BlockSpec((G * BANDS, SLAB, HG_WIDTH),
                                lambda b, i: (b, (n_chunks - 1 - i) if rev else i, col))
    else:
        def view(a):
            return a.reshape(T_ALL // CTX_LEN, CTX_LEN, a.shape[-1])

        def spec(col, rev):
            return pl.BlockSpec((G, CHUNK, HG_WIDTH),
                                lambda b, i: (T_LAT // CTX_LEN // G + b, (n_chunks - 1 - i) if rev else i, col))

    p_view = view(p_all)
    o_shape = jax.ShapeDtypeStruct(p_view.shape[:-1] + (HG_WIDTH,), F32)
    state_spec = pl.BlockSpec((G, 2, HG_HEADS, HG_D, HG_D), lambda b, i: (b, 0, 0, 0, 0))
    in_specs = [spec(COL_Q, False), spec(COL_FF, False), spec(COL_V, False),
                spec(COL_Q, True), spec(COL_FB, True), spec(COL_V, True), state_spec]
    args = [p_view] * 6 + [s0]
    aliases = {}
    if out_bufs is not None:
        in_specs += [pl.BlockSpec(memory_space=pl.ANY)] * 2
        args += [view(buf) for buf in out_bufs]
        aliases = {len(args) - 2: 0, len(args) - 1: 1}

    o_f, o_b, s_fin = pl.pallas_call(
        _hgrn_kernel,
        grid=(BATCH // G, n_chunks),
        in_specs=in_specs,
        out_specs=[spec(0, False), spec(0, True), state_spec],
        out_shape=[o_shape, o_shape, jax.ShapeDtypeStruct((BATCH, 2, HG_HEADS, HG_D, HG_D), F32)],
        scratch_shapes=[pltpu.VMEM((G, 2, HG_HEADS, HG_D, HG_D), F32)],
        input_output_aliases=aliases,
        compiler_params=_params("parallel", "arbitrary"),
        name="hgrn2_%s" % ("latent" if latent else "context"),
    )(*args)
    return o_f.reshape(T_ALL, HG_WIDTH), o_b.reshape(T_ALL, HG_WIDTH), s_fin


def _hgrn_mixer(p_all):
    zero_s = jnp.zeros((BATCH, 2, HG_HEADS, HG_D, HG_D), F32)
    o_f, o_b, s_ctx = _hgrn_call(p_all, zero_s, latent=False)
    o_f, o_b, _ = _hgrn_call(p_all, s_ctx, latent=True, out_bufs=(o_f, o_b))
    return o_f, o_b


def _rg_gates(xc, wa_ref, ba_ref, wx_ref, bx_ref, lam_ref):
    xcb = xc.astype(BF16)
    r_parts, i_parts = [], []
    for n in range(RG_BLOCKS):
        sl = slice(n * RG_BW, (n + 1) * RG_BW)
        r_parts.append(jnp.dot(xcb[:, sl], wa_ref[n], preferred_element_type=F32))
        i_parts.append(jnp.dot(xcb[:, sl], wx_ref[n], preferred_element_type=F32))
    r = _sigmoid(jnp.concatenate(r_parts, axis=1) + ba_ref[...])
    ig = _sigmoid(jnp.concatenate(i_parts, axis=1) + bx_ref[...])
    nl = -lam_ref[...]
    softplus = jnp.maximum(nl, 0.0) + jnp.log1p(jnp.exp(-jnp.abs(nl)))
    log_a = (-RG_C) * r * softplus
    a = jnp.exp(log_a)
    return a, jnp.sqrt(-jnp.tanh(log_a) * (1.0 + a * a)) * ig * xc


def _rg_ctx_kernel(*refs, rev, final):
    x_ref, cw_ref, cb_ref, wa_ref, ba_ref, wx_ref, bx_ref, lam_ref = refs[:8]
    if final:
        hb_ref, gate_ref, o_ref, hfin_ref = refs[8:]
    else:
        o_ref, hfin_ref = refs[8:]
    R = CTX_LEN
    N = R + 2 * SUBLANES
    x = x_ref[...]
    pad = jnp.zeros((SUBLANES, RG_WIDTH), F32)
    ext = jnp.concatenate([pad, x, pad], axis=0)
    lo, hi_ = SUBLANES, SUBLANES + R
    xc = (cw_ref[0:1, :] * pltpu.roll(ext, 2, 0)[lo:hi_]
          + cw_ref[1:2, :] * pltpu.roll(ext, 1, 0)[lo:hi_]
          + cw_ref[2:3, :] * x
          + cw_ref[3:4, :] * pltpu.roll(ext, N - 1, 0)[lo:hi_]
          + cb_ref[...])
    a, bt = _rg_gates(xc, wa_ref, ba_ref, wx_ref, bx_ref, lam_ref)

    row = lax.broadcasted_iota(jnp.int32, (R, RG_WIDTH), 0)
    step = 1
    while step < R:
        if rev:
            keep = row < R - step
            shift = R - step
        else:
            keep = row >= step
            shift = step
        a_sh = jnp.where(keep, pltpu.roll(a, shift, 0), 1.0)
        b_sh = jnp.where(keep, pltpu.roll(bt, shift, 0), 0.0)
        bt = bt + a * b_sh
        a = a * a_sh
        step *= 2
    h = bt
    end = 0 if rev else R - 1
    hfin_ref[...] = h[end:end + 1, :]
    if final:
        o_ref[...] = ((h + hb_ref[...]) * gate_ref[...]).astype(o_ref.dtype)
    else:
        o_ref[...] = h


def _rg_lat_kernel(*refs, rev, final):
    xp_ref, x_ref, xn_ref, cw_ref, cb_ref, wa_ref, ba_ref, wx_ref, bx_ref, lam_ref, h0_ref = refs[:11]
    if final:
        hb_ref, gate_ref, _alias, o_ref, ext, a_scr, h_scr, carry = refs[11:]
    else:
        _alias, o_ref, ext, a_scr, h_scr, carry = refs[11:]
    S = SLAB
    j = pl.program_id(1)
    band = (BANDS - 1 - j) if rev else j

    @pl.when(j == 0)
    def _():
        carry[...] = h0_ref[...]

    row = lax.broadcasted_iota(jnp.int32, (S, RG_WIDTH), 0)
    has_prev = (band > 0).astype(F32)
    has_next = (band < BANDS - 1).astype(F32)

    def shift_down(slab, top):
        return jnp.where(row == 0, top, pltpu.roll(slab, 1, 0))

    def shift_up(slab, bottom):
        return jnp.where(row == S - 1, bottom, pltpu.roll(slab, S - 1, 0))

    ext[0:S, :] = shift_down(x_ref[BAND - 2 * S:BAND - S, :], xp_ref[S - 1:S, :] * has_prev)
    ext[S:2 * S, :] = shift_down(x_ref[BAND - S:BAND, :], xp_ref[2 * S - 1:2 * S, :] * has_prev)
    ext[2 * S:2 * S + BAND, :] = x_ref[...]
    ext[2 * S + BAND:3 * S + BAND, :] = shift_up(x_ref[0:S, :], xn_ref[0:1, :] * has_next)

    CH = 256
    for s in range(0, BAND, CH):
        xc = (cw_ref[0:1, :] * ext[s:s + CH, :]
              + cw_ref[1:2, :] * ext[s + S:s + S + CH, :]
              + cw_ref[2:3, :] * ext[s + 2 * S:s + 2 * S + CH, :]
              + cw_ref[3:4, :] * ext[s + 3 * S:s + 3 * S + CH, :]
              + cb_ref[...])
        a, bt = _rg_gates(xc, wa_ref, ba_ref, wx_ref, bx_ref, lam_ref)
        a_scr[s:s + CH, :] = a
        h_scr[s:s + CH, :] = bt

    def body(t, hc):
        h, ac = hc
        c = (GRID_W - 1 - t) if rev else t
        sl = pl.ds(pl.multiple_of(c * S, S), S)
        a_c = a_scr[sl, :]
        h = a_c * h + h_scr[sl, :]
        ac = a_c * ac
        h_scr[sl, :] = h
        a_scr[sl, :] = ac
        return h, ac

    h_end, a_end = lax.fori_loop(0, GRID_W, body,
                                 (jnp.zeros((S, RG_WIDTH), F32), jnp.ones((S, RG_WIDTH), F32)),
                                 unroll=4)
    cin = jnp.zeros((S, RG_WIDTH), F32)
    c_run = carry[...]
    order = range(S - 1, -1, -1) if rev else range(S)
    for r in order:
        cin = jnp.where(row == r, c_run, cin)
        c_run = h_end[r:r + 1, :] + a_end[r:r + 1, :] * c_run
    carry[...] = c_run
    h = (h_scr[...].reshape(GRID_W, S, RG_WIDTH)
         + a_scr[...].reshape(GRID_W, S, RG_WIDTH) * cin[None]).reshape(BAND, RG_WIDTH)
    if final:
        o_ref[...] = ((h + hb_ref[...]) * gate_ref[...]).astype(o_ref.dtype)
    else:
        o_ref[...] = h


def _rg_call(p_all, w, *, rev, h_back=None):
    cw, cb, wa, ba, wx, bx, lam = w
    final = h_back is not None
    out_dtype = BF16 if final else F32

    def const(shape):
        return pl.BlockSpec(shape, lambda *_: (0,) * len(shape))

    w_specs = [const((4, RG_WIDTH)), const((1, RG_WIDTH)),
               const((RG_BLOCKS, RG_BW, RG_BW)), const((1, RG_WIDTH)),
               const((RG_BLOCKS, RG_BW, RG_BW)), const((1, RG_WIDTH)), const((1, RG_WIDTH))]
    w_args = [cw, cb, wa, ba, wx, bx, lam]
    carry_spec = pl.BlockSpec((None, 1, RG_WIDTH), lambda b, *_: (b, 0, 0))
    carry_shape = jax.ShapeDtypeStruct((BATCH, 1, RG_WIDTH), F32)

    def cmain(col):
        return pl.BlockSpec((CTX_LEN, RG_WIDTH), lambda b: (T_LAT // CTX_LEN + b, col))

    in_specs = [cmain(COL_RX)] + w_specs
    args = [p_all] + w_args
    if final:
        in_specs += [cmain(0), cmain(COL_RGATE)]
        args += [h_back, p_all]
    out_c, h_ctx = pl.pallas_call(
        functools.partial(_rg_ctx_kernel, rev=rev, final=final),
        grid=(BATCH,),
        in_specs=in_specs,
        out_specs=[cmain(0), carry_spec],
        out_shape=[jax.ShapeDtypeStruct((T_ALL, RG_WIDTH), out_dtype), carry_shape],
        compiler_params=_params("parallel"),
        name="rglru_context_%s" % ("bwd" if rev else "fwd"),
    )(*args)

    def band_idx(b, j):
        return b * BANDS + ((BANDS - 1 - j) if rev else j)

    def lmain(col):
        return pl.BlockSpec((BAND, RG_WIDTH), lambda b, j: (band_idx(b, j), col))

    per = BAND // (2 * SLAB)
    prev_spec = pl.BlockSpec((2 * SLAB, RG_WIDTH),
                             lambda b, j: (jnp.maximum(band_idx(b, j) * per - 1, 0), COL_RX))
    next_spec = pl.BlockSpec((SLAB, RG_WIDTH),
                             lambda b, j: ((band_idx(b, j) + 1) * (BAND // SLAB), COL_RX))
    in_specs = [prev_spec, lmain(COL_RX), next_spec] + w_specs + [carry_spec]
    args = [p_all, p_all, p_all] + w_args + [h_ctx]
    if final:
        in_specs += [lmain(0), lmain(COL_RGATE)]
        args += [h_back, p_all]
    in_specs.append(pl.BlockSpec(memory_space=pl.ANY))
    args.append(out_c)
    return pl.pallas_call(
        functools.partial(_rg_lat_kernel, rev=rev, final=final),
        grid=(BATCH, BANDS),
        in_specs=in_specs,
        out_specs=lmain(0),
        out_shape=jax.ShapeDtypeStruct((T_ALL, RG_WIDTH), out_dtype),
        scratch_shapes=[pltpu.VMEM((BAND + 3 * SLAB, RG_WIDTH), F32),
                        pltpu.VMEM((BAND, RG_WIDTH), F32),
                        pltpu.VMEM((BAND, RG_WIDTH), F32),
                        pltpu.VMEM((1, RG_WIDTH), F32)],
        input_output_aliases={len(args) - 1: 0},
        compiler_params=_params("parallel", "arbitrary"),
        name="rglru_latent_%s" % ("bwd" if rev else "fwd"),
    )(*args)


def _rg_mixer(p_all, cw, cb, wa, ba, wx, bx, lam):
    row = lambda t: t.reshape(1, RG_WIDTH)

    def weights(d):
        return (cw, row(cb), wa[d].astype(BF16), row(ba[d]), wx[d].astype(BF16), row(bx[d]), row(lam[d]))

    h_b = _rg_call(p_all, weights(1), rev=True)
    return _rg_call(p_all, weights(0), rev=False, h_back=h_b)


def _outproj_kernel(of_ref, ob_ref, sg_ref, hgg_ref, rg_ref, w_ref, x_ref, mod_ref, g_ref, o_ref):
    o = of_ref[...] + ob_ref[...]
    gain = hgg_ref[...] * sg_ref[...]
    parts = []
    for h in range(HG_HEADS):
        sl = slice(h * HG_D, (h + 1) * HG_D)
        o_h = o[:, sl]
        parts.append((o_h * _rms_scale(o_h) * gain[:, sl]).astype(BF16))
    hg = jnp.concatenate(parts, axis=1)
    mix = (jnp.dot(hg, w_ref[0:HG_WIDTH, :], preferred_element_type=F32)
           + jnp.dot(rg_ref[...], w_ref[HG_WIDTH:, :], preferred_element_type=F32))
    y = mix * _rms_scale(mix) * g_ref[1:2, :]
    o_ref[...] = x_ref[...] + mod_ref[2:3, :] * y


def _outproj(o_f, o_b, p_all, hgg, rg, w_out, l, x_all, mod_l, norm_g_l, n_rows):
    tm = 512
    row = lambda col: pl.BlockSpec((tm, HG_WIDTH), lambda i: (i, col))
    return pl.pallas_call(
        _outproj_kernel,
        grid=(n_rows // tm,),
        in_specs=[
            row(0), row(0), row(COL_G),
            pl.BlockSpec((1, HG_WIDTH), lambda i: (0, 0)),
            row(0),
            pl.BlockSpec((None, D_MODEL, D_MODEL), lambda i: (l, 0, 0), pipeline_mode=pl.Buffered(1)),
            pl.BlockSpec((tm, D_MODEL), lambda i: (i, 0)),
            pl.BlockSpec((None, 6, D_MODEL), lambda i: (_mod_row(i, tm), 0, 0)),
            pl.BlockSpec((4, D_MODEL), lambda i: (0, 0)),
        ],
        out_specs=pl.BlockSpec((tm, D_MODEL), lambda i: (i, 0)),
        out_shape=jax.ShapeDtypeStruct((T_ALL, D_MODEL), F32),
        input_output_aliases={6: 0},
        compiler_params=_params("parallel"),
        name="out_projection",
    )(o_f, o_b, p_all, hgg, rg, w_out, x_all, mod_l, norm_g_l)


FFN_TM = BAND
FFN_HALO = SLAB


def _ffn_up_kernel(*refs, banded, tile0):
    xp_ref, xm_ref, xn_ref, mod_ref, g_ref, wa_ref, wv_ref, cwa_ref, cwv_ref, cba_ref, cbv_ref = refs[:11]
    o_ref, h_ref = refs[-2:]
    i = pl.program_id(0) + tile0
    n = pl.program_id(1)
    tm, H, S = FFN_TM, FFN_HALO, SLAB
    N = tm + 2 * H

    @pl.when(n == 0)
    def _():
        def norm_mod(x):
            y = x * _rms_scale(x) * g_ref[2:3, :]
            return (y * (1.0 + mod_ref[4:5, :]) + mod_ref[3:4, :]).astype(BF16)
        h_ref[0:H, :] = norm_mod(xp_ref[...])
        h_ref[H:H + tm, :] = norm_mod(xm_ref[...])
        h_ref[H + tm:N, :] = norm_mod(xn_ref[...])

    if banded:
        band = i % BANDS
        has_prev = (band > 0).astype(F32)
        has_next = (band < BANDS - 1).astype(F32)
        row = lax.broadcasted_iota(jnp.int32, (S, o_ref.shape[1]), 0)

        def conv(u, cw, cb):
            mid = u[H:H + tm]
            first = jnp.where(row == 0, u[H - 1:H] * has_prev, pltpu.roll(mid[tm - S:tm], 1, 0))
            last = jnp.where(row == S - 1, u[H + tm:H + tm + 1] * has_next, pltpu.roll(mid[0:S], S - 1, 0))
            prev = jnp.concatenate([first, mid[0:tm - S]], axis=0)
            nxt = jnp.concatenate([mid[S:tm], last], axis=0)
            return cw[0:1] * prev + cw[1:2] * mid + cw[2:3] * nxt + cb
    else:
        pos = lax.broadcasted_iota(jnp.int32, (tm, 1), 0) % CTX_LEN
        has_prev = (pos != 0).astype(F32)
        has_next = (pos != CTX_LEN - 1).astype(F32)

        def conv(u, cw, cb):
            return (cw[0:1] * (pltpu.roll(u, 1, 0)[H:H + tm] * has_prev)
                    + cw[1:2] * u[H:H + tm]
                    + cw[2:3] * (pltpu.roll(u, N - 1, 0)[H:H + tm] * has_next)
                    + cb)

    hh = h_ref[...]
    ua = conv(jnp.dot(hh, wa_ref[...], preferred_element_type=F32), cwa_ref[...], cba_ref[...])
    uv = conv(jnp.dot(hh, wv_ref[...], preferred_element_type=F32), cwv_ref[...], cbv_ref[...])
    o_ref[...] = (_gelu_tanh(ua) * uv).astype(o_ref.dtype)


def _ffn_up_call(x_all, mod_l, norm_g_l, w_up, l, cw_l, cb2, *, banded, tile0, n_tiles, n_rows, out_buf):
    tm, tn = FFN_TM, 512
    nn = D_FF // tn
    per = tm // FFN_HALO
    last = T_ALL // FFN_HALO - 1
    in_specs = [
        pl.BlockSpec((FFN_HALO, D_MODEL), lambda i, n: (jnp.maximum((i + tile0) * per - 1, 0), 0)),
        pl.BlockSpec((tm, D_MODEL), lambda i, n: (i + tile0, 0)),
        pl.BlockSpec((FFN_HALO, D_MODEL), lambda i, n: (jnp.minimum((i + tile0 + 1) * per, last), 0)),
        pl.BlockSpec((None, 6, D_MODEL), lambda i, n: (_mod_row(i + tile0, tm), 0, 0)),
        pl.BlockSpec((4, D_MODEL), lambda i, n: (0, 0)),
        pl.BlockSpec((None, D_MODEL, tn), lambda i, n: (l, 0, n)),
        pl.BlockSpec((None, D_MODEL, tn), lambda i, n: (l, 0, nn + n)),
        pl.BlockSpec((3, tn), lambda i, n: (0, n)),
        pl.BlockSpec((3, tn), lambda i, n: (0, nn + n)),
        pl.BlockSpec((1, tn), lambda i, n: (0, n)),
        pl.BlockSpec((1, tn), lambda i, n: (0, nn + n)),
    ]
    args = [x_all, x_all, x_all, mod_l, norm_g_l, w_up, w_up, cw_l, cw_l, cb2, cb2]
    aliases = {}
    if out_buf is not None:
        in_specs.append(pl.BlockSpec(memory_space=pl.ANY))
        args.append(out_buf)
        aliases = {len(args) - 1: 0}
    return pl.pallas_call(
        functools.partial(_ffn_up_kernel, banded=banded, tile0=tile0),
        grid=(n_tiles, nn),
        in_specs=in_specs,
        out_specs=pl.BlockSpec((tm, tn), lambda i, n: (i + tile0, n)),
        out_shape=jax.ShapeDtypeStruct((n_rows, D_FF), BF16),
        scratch_shapes=[pltpu.VMEM((tm + 2 * FFN_HALO, D_MODEL), BF16)],
        input_output_aliases=aliases,
        compiler_params=_params("parallel", "arbitrary"),
        name="ffn_up_%s" % ("latent" if banded else "context"),
    )(*args)


def _ffn_up(x_all, mod_l, norm_g_l, w_up, l, cw_l, cb_l, n_rows):
    cb2 = cb_l.reshape(1, 2 * D_FF)
    act = _ffn_up_call(x_all, mod_l, norm_g_l, w_up, l, cw_l, cb2, banded=True, tile0=0,
                       n_tiles=T_LAT // FFN_TM, n_rows=n_rows, out_buf=None)
    if n_rows > T_LAT:
        act = _ffn_up_call(x_all, mod_l, norm_g_l, w_up, l, cw_l, cb2, banded=False,
                           tile0=T_LAT // FFN_TM, n_tiles=T_CTX // FFN_TM, n_rows=n_rows, out_buf=act)
    return act


def _ffn_down_kernel(a_ref, w_ref, x_ref, mod_ref, g_ref, o_ref):
    f = jnp.dot(a_ref[...], w_ref[...], preferred_element_type=F32)
    y = f * _rms_scale(f) * g_ref[3:4, :]
    o_ref[...] = x_ref[...] + mod_ref[5:6, :] * y


def _ffn_down(act, w_down, l, x_all, mod_l, norm_g_l, n_rows, in_place):
    tm = 256
    return pl.pallas_call(
        _ffn_down_kernel,
        grid=(n_rows // tm,),
        in_specs=[
            pl.BlockSpec((tm, D_FF), lambda i: (i, 0)),
            pl.BlockSpec((None, D_FF, D_MODEL), lambda i: (l, 0, 0), pipeline_mode=pl.Buffered(1)),
            pl.BlockSpec((tm, D_MODEL), lambda i: (i, 0)),
            pl.BlockSpec((None, 6, D_MODEL), lambda i: (_mod_row(i, tm), 0, 0)),
            pl.BlockSpec((4, D_MODEL), lambda i: (0, 0)),
        ],
        out_specs=pl.BlockSpec((tm, D_MODEL), lambda i: (i, 0)),
        out_shape=jax.ShapeDtypeStruct((T_ALL if in_place else n_rows, D_MODEL), F32),
        input_output_aliases={2: 0} if in_place else {},
        compiler_params=_params("parallel"),
        name="ffn_down",
    )(act, w_down, x_all, mod_l, norm_g_l)


def kernel(x, c, ctx, c_ctx, w_mod, b_mod, norm_g, w_in, hg_lower_bounds, hg_norm_g, rg_conv_w,
           rg_conv_b, rg_wa, rg_ba, rg_wx, rg_bx, rg_lambda, w_out, ffn_w_up, ffn_conv_w,
           ffn_conv_b, ffn_w_down):
    x_all = jnp.concatenate([_to_banded(x), ctx.reshape(T_CTX, D_MODEL)], axis=0)
    cc = jnp.concatenate([c, c_ctx[None, :], jnp.zeros((MOD_ROWS - BATCH - 1, D_MODEL), F32)], axis=0)
    mods = _modulation(cc, w_mod, b_mod)
    lb_all = _lower_bounds(hg_lower_bounds)

    w_in, w_out, ffn_w_up, ffn_w_down = (w.astype(BF16) for w in (w_in, w_out, ffn_w_up, ffn_w_down))

    for l in range(DEPTH):
        ctx_out = l < DEPTH - 1
        n_rows = T_ALL if ctx_out else T_LAT
        p_all = _inproj(x_all, mods[l], norm_g[l], lb_all[l], w_in, l)
        o_f, o_b = _hgrn_mixer(p_all)
        rg = _rg_mixer(p_all, rg_conv_w[l], rg_conv_b[l], rg_wa[l], rg_ba[l], rg_wx[l], rg_bx[l],
                       rg_lambda[l])
        x_all = _outproj(o_f, o_b, p_all, hg_norm_g[l].reshape(1, HG_WIDTH), rg, w_out, l, x_all,
                         mods[l], norm_g[l], n_rows)
        act = _ffn_up(x_all, mods[l], norm_g[l], ffn_w_up, l, ffn_conv_w[l], ffn_conv_b[l], n_rows)
        x_all = _ffn_down(act, ffn_w_down, l, x_all, mods[l], norm_g[l], n_rows, in_place=ctx_out)
    return _from_banded(x_all)
```

```python
import functools
import math

import jax
import jax.numpy as jnp
from jax import lax
from jax.experimental import pallas as pl
from jax.experimental.pallas import tpu as pltpu

F32 = jnp.float32
BF16 = jnp.bfloat16

D_MODEL = 2048
BATCH = 4
SEQ = 4096
DEPTH = 4
GRID_W = 64
GRID_ROWS = SEQ // GRID_W
CTX_LEN = 256
HG_WIDTH = 1024
HG_HEADS = 8
HG_D = HG_WIDTH // HG_HEADS
CHUNK = 64
RG_WIDTH = 1024
RG_BLOCKS = 8
RG_BW = RG_WIDTH // RG_BLOCKS
RG_C = 8.0
D_FF = 5632
N_PROJ = 5 * HG_WIDTH + 2 * RG_WIDTH
EPS = 1e-6

T_LAT = BATCH * SEQ
T_CTX = BATCH * CTX_LEN
T_ALL = T_LAT + T_CTX
MOD_ROWS = 8
SUBLANES = 8
SLAB = 16
BAND = SLAB * GRID_W
BANDS = SEQ // BAND
EXP_CLAMP = 80.0
HG_SEQS = 2
OUTPROJ_SUBTILES = 2

VMEM_LIMIT = 56 * 1024 * 1024

COL_Q, COL_FF, COL_FB, COL_V, COL_G, COL_RX, COL_RGATE = range(7)

NT_DIMS = (((1,), (1,)), ((), ()))
TN_DIMS = (((0,), (0,)), ((), ()))


def _sigmoid(x):
    return 0.5 + 0.5 * jnp.tanh(0.5 * x)


def _silu(x):
    return x * _sigmoid(x)


def _gelu_tanh(x):
    c = math.sqrt(2.0 / math.pi)
    return x * (0.5 * (1.0 + jnp.tanh(c * (x + 0.044715 * (x * x * x)))))


def _rms_scale(x):
    return lax.rsqrt(jnp.mean(x * x, axis=-1, keepdims=True) + EPS)


def _params(*sem):
    return pltpu.CompilerParams(dimension_semantics=sem, vmem_limit_bytes=VMEM_LIMIT)


def _mod_row(i, tm):
    return jnp.minimum(i // (SEQ // tm), BATCH)


def _to_banded(x):
    t = x.reshape(BATCH, BANDS, SLAB, GRID_W, D_MODEL)
    return t.transpose(0, 1, 3, 2, 4).reshape(T_LAT, D_MODEL)


def _from_banded(y):
    t = y.reshape(BATCH, BANDS, GRID_W, SLAB, D_MODEL)
    return t.transpose(0, 1, 3, 2, 4).reshape(BATCH, SEQ, D_MODEL)


def _lb_kernel(x_ref, o_ref):
    x = x_ref[...]
    m = jnp.max(x, axis=0, keepdims=True)
    e = jnp.exp(x - m)
    sm = e / jnp.sum(e, axis=0, keepdims=True)
    acc = jnp.zeros_like(sm[0:1])
    o_ref[0:1, :] = acc
    for l in range(1, DEPTH):
        acc = acc + sm[l:l + 1]
        o_ref[l:l + 1, :] = acc


def _lower_bounds(hg_lower_bounds):
    x = hg_lower_bounds.reshape(DEPTH, 2 * HG_WIDTH)
    out = pl.pallas_call(
        _lb_kernel,
        out_shape=jax.ShapeDtypeStruct((DEPTH, 2 * HG_WIDTH), F32),
        name="hg_lower_bounds",
    )(x)
    return out.reshape(DEPTH, 2, HG_WIDTH)


def _mod_kernel(cc_ref, w_ref, b_ref, o_ref):
    k = pl.program_id(1)

    @pl.when(k == 0)
    def _():
        o_ref[...] = jnp.broadcast_to(b_ref[...], o_ref.shape)

    s = _silu(cc_ref[...])
    o_ref[...] += jnp.dot(s.astype(BF16), w_ref[...].astype(BF16), preferred_element_type=F32)


def _modulation(cc, w_mod, b_mod):
    tk = 256
    n_out = 6 * D_MODEL
    out = pl.pallas_call(
        _mod_kernel,
        grid=(DEPTH, D_MODEL // tk),
        in_specs=[
            pl.BlockSpec((MOD_ROWS, tk), lambda l, k: (0, k)),
            pl.BlockSpec((None, tk, n_out), lambda l, k: (l, k, 0)),
            pl.BlockSpec((None, 1, n_out), lambda l, k: (l, 0, 0)),
        ],
        out_specs=pl.BlockSpec((None, MOD_ROWS, n_out), lambda l, k: (l, 0, 0)),
        out_shape=jax.ShapeDtypeStruct((DEPTH, MOD_ROWS, n_out), F32),
        compiler_params=_params("parallel", "arbitrary"),
        name="adaln_modulation",
    )(cc, w_mod, b_mod.reshape(DEPTH, 1, n_out))
    return out.reshape(DEPTH, MOD_ROWS, 6, D_MODEL)


def _inproj_kernel(x_ref, mod_ref, g_ref, lb_ref, w_ref, o_ref):
    x = x_ref[...]
    y = x * _rms_scale(x) * g_ref[0:1, :]
    h = (y * (1.0 + mod_ref[1:2, :]) + mod_ref[0:1, :]).astype(BF16)

    def gate(lb):
        return lambda z: lb + (1.0 - lb) * _sigmoid(z)

    acts = {COL_Q: _silu, COL_FF: gate(lb_ref[0:1, :]), COL_FB: gate(lb_ref[1:2, :]), COL_V: None,
            COL_G: _silu, COL_RX: None, COL_RGATE: _gelu_tanh}
    for col, act in acts.items():
        cs = slice(col * HG_WIDTH, (col + 1) * HG_WIDTH)
        z = jnp.dot(h, w_ref[:, cs], preferred_element_type=F32)
        o_ref[:, cs] = z if act is None else act(z)


def _inproj(x_all, mod_l, norm_g_l, lb_l, w_in, l):
    tm = 256
    return pl.pallas_call(
        _inproj_kernel,
        grid=(T_ALL // tm,),
        in_specs=[
            pl.BlockSpec((tm, D_MODEL), lambda i: (i, 0)),
            pl.BlockSpec((None, 6, D_MODEL), lambda i: (_mod_row(i, tm), 0, 0)),
            pl.BlockSpec((4, D_MODEL), lambda i: (0, 0)),
            pl.BlockSpec((2, HG_WIDTH), lambda i: (0, 0)),
            pl.BlockSpec((None, D_MODEL, N_PROJ), lambda i: (l, 0, 0), pipeline_mode=pl.Buffered(1)),
        ],
        out_specs=pl.BlockSpec((tm, N_PROJ), lambda i: (i, 0)),
        out_shape=jax.ShapeDtypeStruct((T_ALL, N_PROJ), F32),
        compiler_params=_params("parallel"),
        name="in_projection",
    )(x_all, mod_l, norm_g_l, lb_l, w_in)


def _hgrn_chunk(q, f, v, states, rev):
    C = CHUNK
    NB = C // 16
    vb = v.astype(BF16)
    k = 1.0 - f
    g = jnp.log(f)

    def scan_pos(idx):
        return (C - 1 - idx) if rev else idx

    pr = scan_pos(lax.broadcasted_iota(jnp.int32, (C, C), 0))
    pc = scan_pos(lax.broadcasted_iota(jnp.int32, (C, C), 1))
    tri = (pc <= pr).astype(BF16)
    g_hi = g.astype(BF16)
    r1 = g - g_hi.astype(F32)
    g_mid = r1.astype(BF16)
    g_lo = (r1 - g_mid.astype(F32)).astype(BF16)
    b = (jnp.dot(tri, g_hi, preferred_element_type=F32)
         + jnp.dot(tri, g_mid, preferred_element_type=F32)
         + jnp.dot(tri, g_lo, preferred_element_type=F32))

    def rows(p0, p1):
        return slice(C - p1, C - p0) if rev else slice(p0, p1)

    def brow(sp):
        r = scan_pos(sp)
        return b[r:r + 1, :]

    def assemble(blocks):
        zero = jnp.zeros((16, HG_WIDTH), F32)
        blocks = [zero if blk is None else blk for blk in blocks]
        if rev:
            blocks = blocks[::-1]
        return jnp.concatenate(blocks, axis=0).astype(BF16)

    q0b, k0b = [], []
    for j in range(NB):
        R = rows(16 * j, 16 * j + 16)
        d = jnp.clip(b[R] - 0.5 * (brow(16 * j) + brow(16 * j + 15)), -EXP_CLAMP, EXP_CLAMP)
        q0b.append(q[R] * jnp.exp(d))
        k0b.append(k[R] * jnp.exp(-d))
    q0, k0 = assemble(q0b), assemble(k0b)
    q1b, k1b = [], []
    for m in range(NB // 2):
        ref = brow(32 * m + 15)
        Re, Ro = rows(32 * m, 32 * m + 16), rows(32 * m + 16, 32 * m + 32)
        k1b += [k[Re] * jnp.exp(ref - b[Re]), None]
        q1b += [None, q[Ro] * jnp.exp(b[Ro] - ref)]
    q1, k1 = assemble(q1b), assemble(k1b)
    ref = brow(31)
    q2b = [None, None] + [q[rows(16 * j, 16 * j + 16)] * jnp.exp(b[rows(16 * j, 16 * j + 16)] - ref)
                          for j in (2, 3)]
    k2b = [k[rows(16 * j, 16 * j + 16)] * jnp.exp(ref - b[rows(16 * j, 16 * j + 16)])
           for j in (0, 1)] + [None, None]
    q2, k2 = assemble(q2b), assemble(k2b)
    b_last = brow(C - 1)
    q3 = (q * jnp.exp(b)).astype(BF16)
    k3 = (k * jnp.exp(b_last - b)).astype(BF16)
    dec = jnp.exp(b_last)

    m0 = ((pr >> 4) == (pc >> 4)) & (pc <= pr)
    m1 = ((pr >> 5) == (pc >> 5)) & (((pr >> 4) & 1) == 1) & (((pc >> 4) & 1) == 0)

    heads = [slice(h * HG_D, (h + 1) * HG_D) for h in range(HG_HEADS)]
    stage1 = []
    for h, sl in enumerate(heads):
        a0 = lax.dot_general(q0[:, sl], k0[:, sl], NT_DIMS, preferred_element_type=F32)
        a1 = lax.dot_general(q1[:, sl], k1[:, sl], NT_DIMS, preferred_element_type=F32)
        a2 = lax.dot_general(q2[:, sl], k2[:, sl], NT_DIMS, preferred_element_type=F32)
        inter = lax.dot_general(q3[:, sl], states[h].astype(BF16), NT_DIMS, preferred_element_type=F32)
        upd = lax.dot_general(vb[:, sl], k3[:, sl], TN_DIMS, preferred_element_type=F32)
        stage1.append((a0, a1, a2, inter, upd))
    outs, new_states = [], []
    for h, sl in enumerate(heads):
        a0, a1, a2, inter, upd = stage1[h]
        scores = jnp.where(m0, a0, jnp.where(m1, a1, a2))
        outs.append(jnp.dot(scores.astype(BF16), vb[:, sl], preferred_element_type=F32) + inter)
        new_states.append(states[h] * dec[:, sl] + upd)
    return jnp.concatenate(outs, axis=1), new_states


def _hgrn_kernel(qf_ref, ff_ref, vf_ref, qb_ref, fb_ref, vb_ref, s0_ref, *rest):
    of_ref, ob_ref, sfin_ref, s_scr = rest[-4:]
    i = pl.program_id(1)

    @pl.when(i == 0)
    def _():
        s_scr[...] = s0_ref[...]

    def chunk_rows(ref, s):
        return ref[...].reshape(HG_SEQS, CHUNK, HG_WIDTH)[s]

    results = {}
    for s in range(HG_SEQS):
        for d, (q_ref, f_ref, v_ref) in enumerate(((qf_ref, ff_ref, vf_ref), (qb_ref, fb_ref, vb_ref))):
            states = [s_scr[s, d, h] for h in range(HG_HEADS)]
            results[s, d] = _hgrn_chunk(chunk_rows(q_ref, s), chunk_rows(f_ref, s), chunk_rows(v_ref, s),
                                        states, rev=(d == 1))
    for d, o_ref in enumerate((of_ref, ob_ref)):
        for s in range(HG_SEQS):
            for h in range(HG_HEADS):
                s_scr[s, d, h] = results[s, d][1][h]
        o_ref[...] = jnp.stack([results[s, d][0] for s in range(HG_SEQS)]).reshape(o_ref.shape)

    @pl.when(i == pl.num_programs(1) - 1)
    def _():
        sfin_ref[...] = s_scr[...]


def _hgrn_call(p_all, s0, *, latent, out_bufs=None):
    n_chunks = (SEQ if latent else CTX_LEN) // CHUNK
    G = HG_SEQS

    if latent:
        def view(a):
            return a.reshape(T_ALL // BAND, BAND, a.shape[-1])

        def spec(col, rev):
            return pl.BlockSpec((G * BANDS, SLAB, HG_WIDTH),
                                lambda b, i: (b, (n_chunks - 1 - i) if rev else i, col))
    else:
        def view(a):
            return a.reshape(T_ALL // CTX_LEN, CTX_LEN, a.shape[-1])

        def spec(col, rev):
            return pl.BlockSpec((G, CHUNK, HG_WIDTH),
                                lambda b, i: (T_LAT // CTX_LEN // G + b, (n_chunks - 1 - i) if rev else i, col))

    p_view = view(p_all)
    o_shape = jax.ShapeDtypeStruct(p_view.shape[:-1] + (HG_WIDTH,), F32)
    state_spec = pl.BlockSpec((G, 2, HG_HEADS, HG_D, HG_D), lambda b, i: (b, 0, 0, 0, 0))
    in_specs = [spec(COL_Q, False), spec(COL_FF, False), spec(COL_V, False),
                spec(COL_Q, True), spec(COL_FB, True), spec(COL_V, True), state_spec]
    args = [p_view] * 6 + [s0]
    aliases = {}
    if out_bufs is not None:
        in_specs += [pl.BlockSpec(memory_space=pl.ANY)] * 2
        args += [view(buf) for buf in out_bufs]
        aliases = {len(args) - 2: 0, len(args) - 1: 1}

    o_f, o_b, s_fin = pl.pallas_call(
        _hgrn_kernel,
        grid=(BATCH // G, n_chunks),
        in_specs=in_specs,
        out_specs=[spec(0, False), spec(0, True), state_spec],
        out_shape=[o_shape, o_shape, jax.ShapeDtypeStruct((BATCH, 2, HG_HEADS, HG_D, HG_D), F32)],
        scratch_shapes=[pltpu.VMEM((G, 2, HG_HEADS, HG_D, HG_D), F32)],
        input_output_aliases=aliases,
        compiler_params=_params("parallel", "arbitrary"),
        name="hgrn2_%s" % ("latent" if latent else "context"),
    )(*args)
    return o_f.reshape(T_ALL, HG_WIDTH), o_b.reshape(T_ALL, HG_WIDTH), s_fin


def _hgrn_mixer(p_all):
    zero_s = jnp.zeros((BATCH, 2, HG_HEADS, HG_D, HG_D), F32)
    o_f, o_b, s_ctx = _hgrn_call(p_all, zero_s, latent=False)
    o_f, o_b, _ = _hgrn_call(p_all, s_ctx, latent=True, out_bufs=(o_f, o_b))
    return o_f, o_b


def _rg_gates(xc, wa_ref, ba_ref, wx_ref, bx_ref, lam_ref):
    xcb = xc.astype(BF16)
    r_parts, i_parts = [], []
    for n in range(RG_BLOCKS):
        sl = slice(n * RG_BW, (n + 1) * RG_BW)
        r_parts.append(jnp.dot(xcb[:, sl], wa_ref[n], preferred_element_type=F32))
        i_parts.append(jnp.dot(xcb[:, sl], wx_ref[n], preferred_element_type=F32))
    r = _sigmoid(jnp.concatenate(r_parts, axis=1) + ba_ref[...])
    ig = _sigmoid(jnp.concatenate(i_parts, axis=1) + bx_ref[...])
    nl = -lam_ref[...]
    softplus = jnp.maximum(nl, 0.0) + jnp.log1p(jnp.exp(-jnp.abs(nl)))
    log_a = (-RG_C) * r * softplus
    a = jnp.exp(log_a)
    return a, jnp.sqrt(-jnp.tanh(log_a) * (1.0 + a * a)) * ig * xc


def _rg_ctx_kernel(*refs, rev, final):
    x_ref, cw_ref, cb_ref, wa_ref, ba_ref, wx_ref, bx_ref, lam_ref = refs[:8]
    if final:
        hb_ref, gate_ref, o_ref, hfin_ref = refs[8:]
    else:
        o_ref, hfin_ref = refs[8:]
    R = CTX_LEN
    N = R + 2 * SUBLANES
    x = x_ref[...]
    pad = jnp.zeros((SUBLANES, RG_WIDTH), F32)
    ext = jnp.concatenate([pad, x, pad], axis=0)
    lo, hi_ = SUBLANES, SUBLANES + R
    xc = (cw_ref[0:1, :] * pltpu.roll(ext, 2, 0)[lo:hi_]
          + cw_ref[1:2, :] * pltpu.roll(ext, 1, 0)[lo:hi_]
          + cw_ref[2:3, :] * x
          + cw_ref[3:4, :] * pltpu.roll(ext, N - 1, 0)[lo:hi_]
          + cb_ref[...])
    a, bt = _rg_gates(xc, wa_ref, ba_ref, wx_ref, bx_ref, lam_ref)

    row = lax.broadcasted_iota(jnp.int32, (R, RG_WIDTH), 0)
    step = 1
    while step < R:
        if rev:
            keep = row < R - step
            shift = R - step
        else:
            keep = row >= step
            shift = step
        a_sh = jnp.where(keep, pltpu.roll(a, shift, 0), 1.0)
        b_sh = jnp.where(keep, pltpu.roll(bt, shift, 0), 0.0)
        bt = bt + a * b_sh
        a = a * a_sh
        step *= 2
    h = bt
    end = 0 if rev else R - 1
    hfin_ref[...] = h[end:end + 1, :]
    if final:
        o_ref[...] = ((h + hb_ref[...]) * gate_ref[...]).astype(o_ref.dtype)
    else:
        o_ref[...] = h


def _rg_lat_kernel(*refs, rev, final):
    xp_ref, x_ref, xn_ref, cw_ref, cb_ref, wa_ref, ba_ref, wx_ref, bx_ref, lam_ref, h0_ref = refs[:11]
    if final:
        hb_ref, gate_ref, _alias, o_ref, ext, a_scr, h_scr, carry = refs[11:]
    else:
        _alias, o_ref, ext, a_scr, h_scr, carry = refs[11:]
    S = SLAB
    j = pl.program_id(1)
    band = (BANDS - 1 - j) if rev else j

    @pl.when(j == 0)
    def _():
        carry[...] = h0_ref[...]

    row = lax.broadcasted_iota(jnp.int32, (S, RG_WIDTH), 0)
    has_prev = (band > 0).astype(F32)
    has_next = (band < BANDS - 1).astype(F32)

    def shift_down(slab, top):
        return jnp.where(row == 0, top, pltpu.roll(slab, 1, 0))

    def shift_up(slab, bottom):
        return jnp.where(row == S - 1, bottom, pltpu.roll(slab, S - 1, 0))

    ext[0:S, :] = shift_down(x_ref[BAND - 2 * S:BAND - S, :], xp_ref[S - 1:S, :] * has_prev)
    ext[S:2 * S, :] = shift_down(x_ref[BAND - S:BAND, :], xp_ref[2 * S - 1:2 * S, :] * has_prev)
    ext[2 * S:2 * S + BAND, :] = x_ref[...]
    ext[2 * S + BAND:3 * S + BAND, :] = shift_up(x_ref[0:S, :], xn_ref[0:1, :] * has_next)

    CH = 256
    for s in range(0, BAND, CH):
        xc = (cw_ref[0:1, :] * ext[s:s + CH, :]
              + cw_ref[1:2, :] * ext[s + S:s + S + CH, :]
              + cw_ref[2:3, :] * ext[s + 2 * S:s + 2 * S + CH, :]
              + cw_ref[3:4, :] * ext[s + 3 * S:s + 3 * S + CH, :]
              + cb_ref[...])
        a, bt = _rg_gates(xc, wa_ref, ba_ref, wx_ref, bx_ref, lam_ref)
        a_scr[s:s + CH, :] = a
        h_scr[s:s + CH, :] = bt

    def body(t, hc):
        h, ac = hc
        c = (GRID_W - 1 - t) if rev else t
        sl = pl.ds(pl.multiple_of(c * S, S), S)
        a_c = a_scr[sl, :]
        h = a_c * h + h_scr[sl, :]
        ac = a_c * ac
        h_scr[sl, :] = h
        a_scr[sl, :] = ac
        return h, ac

    h_end, a_end = lax.fori_loop(0, GRID_W, body,
                                 (jnp.zeros((S, RG_WIDTH), F32), jnp.ones((S, RG_WIDTH), F32)),
                                 unroll=4)
    cin = jnp.zeros((S, RG_WIDTH), F32)
    c_run = carry[...]
    order = range(S - 1, -1, -1) if rev else range(S)
    for r in order:
        cin = jnp.where(row == r, c_run, cin)
        c_run = h_end[r:r + 1, :] + a_end[r:r + 1, :] * c_run
    carry[...] = c_run
    h = (h_scr[...].reshape(GRID_W, S, RG_WIDTH)
         + a_scr[...].reshape(GRID_W, S, RG_WIDTH) * cin[None]).reshape(BAND, RG_WIDTH)
    if final:
        o_ref[...] = ((h + hb_ref[...]) * gate_ref[...]).astype(o_ref.dtype)
    else:
        o_ref[...] = h


def _rg_call(p_all, w, *, rev, h_back=None):
    cw, cb, wa, ba, wx, bx, lam = w
    final = h_back is not None
    out_dtype = BF16 if final else F32

    def const(shape):
        return pl.BlockSpec(shape, lambda *_: (0,) * len(shape))

    w_specs = [const((4, RG_WIDTH)), const((1, RG_WIDTH)),
               const((RG_BLOCKS, RG_BW, RG_BW)), const((1, RG_WIDTH)),
               const((RG_BLOCKS, RG_BW, RG_BW)), const((1, RG_WIDTH)), const((1, RG_WIDTH))]
    w_args = [cw, cb, wa, ba, wx, bx, lam]
    carry_spec = pl.BlockSpec((None, 1, RG_WIDTH), lambda b, *_: (b, 0, 0))
    carry_shape = jax.ShapeDtypeStruct((BATCH, 1, RG_WIDTH), F32)

    def cmain(col):
        return pl.BlockSpec((CTX_LEN, RG_WIDTH), lambda b: (T_LAT // CTX_LEN + b, col))

    in_specs = [cmain(COL_RX)] + w_specs
    args = [p_all] + w_args
    if final:
        in_specs += [cmain(0), cmain(COL_RGATE)]
        args += [h_back, p_all]
    out_c, h_ctx = pl.pallas_call(
        functools.partial(_rg_ctx_kernel, rev=rev, final=final),
        grid=(BATCH,),
        in_specs=in_specs,
        out_specs=[cmain(0), carry_spec],
        out_shape=[jax.ShapeDtypeStruct((T_ALL, RG_WIDTH), out_dtype), carry_shape],
        compiler_params=_params("parallel"),
        name="rglru_context_%s" % ("bwd" if rev else "fwd"),
    )(*args)

    def band_idx(b, j):
        return b * BANDS + ((BANDS - 1 - j) if rev else j)

    def lmain(col):
        return pl.BlockSpec((BAND, RG_WIDTH), lambda b, j: (band_idx(b, j), col))

    per = BAND // (2 * SLAB)
    prev_spec = pl.BlockSpec((2 * SLAB, RG_WIDTH),
                             lambda b, j: (jnp.maximum(band_idx(b, j) * per - 1, 0), COL_RX))
    next_spec = pl.BlockSpec((SLAB, RG_WIDTH),
                             lambda b, j: ((band_idx(b, j) + 1) * (BAND // SLAB), COL_RX))
    in_specs = [prev_spec, lmain(COL_RX), next_spec] + w_specs + [carry_spec]
    args = [p_all, p_all, p_all] + w_args + [h_ctx]
    if final:
        in_specs += [lmain(0), lmain(COL_RGATE)]
        args += [h_back, p_all]
    in_specs.append(pl.BlockSpec(memory_space=pl.ANY))
    args.append(out_c)
    return pl.pallas_call(
        functools.partial(_rg_lat_kernel, rev=rev, final=final),
        grid=(BATCH, BANDS),
        in_specs=in_specs,
        out_specs=lmain(0),
        out_shape=jax.ShapeDtypeStruct((T_ALL, RG_WIDTH), out_dtype),
        scratch_shapes=[pltpu.VMEM((BAND + 3 * SLAB, RG_WIDTH), F32),
                        pltpu.VMEM((BAND, RG_WIDTH), F32),
                        pltpu.VMEM((BAND, RG_WIDTH), F32),
                        pltpu.VMEM((1, RG_WIDTH), F32)],
        input_output_aliases={len(args) - 1: 0},
        compiler_params=_params("parallel", "arbitrary"),
        name="rglru_latent_%s" % ("bwd" if rev else "fwd"),
    )(*args)


def _rg_mixer(p_all, cw, cb, wa, ba, wx, bx, lam):
    row = lambda t: t.reshape(1, RG_WIDTH)

    def weights(d):
        return (cw, row(cb), wa[d].astype(BF16), row(ba[d]), wx[d].astype(BF16), row(bx[d]), row(lam[d]))

    h_b = _rg_call(p_all, weights(1), rev=True)
    return _rg_call(p_all, weights(0), rev=False, h_back=h_b)


def _outproj_kernel(of_ref, ob_ref, sg_ref, hgg_ref, rg_ref, w_ref, x_ref, mod_ref, g_ref, o_ref,
                    h_ref):
    sub = o_ref.shape[0] // OUTPROJ_SUBTILES
    for t in range(OUTPROJ_SUBTILES):
        rs = slice(t * sub, (t + 1) * sub)
        o = of_ref[rs, :] + ob_ref[rs, :]
        gain = hgg_ref[...] * sg_ref[rs, :]
        parts = []
        for h in range(HG_HEADS):
            sl = slice(h * HG_D, (h + 1) * HG_D)
            o_h = o[:, sl]
            parts.append((o_h * _rms_scale(o_h) * gain[:, sl]).astype(BF16))
        hg = jnp.concatenate(parts, axis=1)
        mix = (jnp.dot(hg, w_ref[0:HG_WIDTH, :], preferred_element_type=F32)
               + jnp.dot(rg_ref[rs, :], w_ref[HG_WIDTH:, :], preferred_element_type=F32))
        y = mix * _rms_scale(mix) * g_ref[1:2, :]
        x_new = x_ref[rs, :] + mod_ref[2:3, :] * y
        o_ref[rs, :] = x_new
        hn = x_new * _rms_scale(x_new) * g_ref[2:3, :]
        h_ref[rs, :] = (hn * (1.0 + mod_ref[4:5, :]) + mod_ref[3:4, :]).astype(BF16)


def _outproj(o_f, o_b, p_all, hgg, rg, w_out, l, x_all, mod_l, norm_g_l, n_rows):
    tm = 512
    row = lambda col: pl.BlockSpec((tm, HG_WIDTH), lambda i: (i, col))
    return pl.pallas_call(
        _outproj_kernel,
        grid=(n_rows // tm,),
        in_specs=[
            row(0), row(0), row(COL_G),
            pl.BlockSpec((1, HG_WIDTH), lambda i: (0, 0)),
            row(0),
            pl.BlockSpec((None, D_MODEL, D_MODEL), lambda i: (l, 0, 0), pipeline_mode=pl.Buffered(1)),
            pl.BlockSpec((tm, D_MODEL), lambda i: (i, 0)),
            pl.BlockSpec((None, 6, D_MODEL), lambda i: (_mod_row(i, tm), 0, 0)),
            pl.BlockSpec((4, D_MODEL), lambda i: (0, 0)),
        ],
        out_specs=[pl.BlockSpec((tm, D_MODEL), lambda i: (i, 0)),
                   pl.BlockSpec((tm, D_MODEL), lambda i: (i, 0))],
        out_shape=[jax.ShapeDtypeStruct((T_ALL, D_MODEL), F32),
                   jax.ShapeDtypeStruct((T_ALL, D_MODEL), BF16)],
        input_output_aliases={6: 0},
        compiler_params=_params("parallel"),
        name="out_projection",
    )(o_f, o_b, p_all, hgg, rg, w_out, x_all, mod_l, norm_g_l)


FFN_TM = BAND
FFN_HALO = SLAB


def _ffn_up_kernel(*refs, banded, tile0):
    hp_ref, hm_ref, hn_ref, wa_ref, wv_ref, cwa_ref, cwv_ref, cba_ref, cbv_ref = refs[:9]
    o_ref, h_ref = refs[-2:]
    i = pl.program_id(0) + tile0
    n = pl.program_id(1)
    tm, H, S = FFN_TM, FFN_HALO, SLAB
    N = tm + 2 * H

    @pl.when(n == 0)
    def _():
        hp, hn = hp_ref[...], hn_ref[...]
        if banded:
            band = i % BANDS
            hp = jnp.where(band > 0, hp, jnp.zeros_like(hp))
            hn = jnp.where(band < BANDS - 1, hn, jnp.zeros_like(hn))
        h_ref[0:H, :] = hp
        h_ref[H:H + tm, :] = hm_ref[...]
        h_ref[H + tm:N, :] = hn

    if banded:
        row = lax.broadcasted_iota(jnp.int32, (S, o_ref.shape[1]), 0)

        def conv(u, cw, cb):
            mid = u[H:H + tm]
            first = jnp.where(row == 0, u[H - 1:H], pltpu.roll(mid[tm - S:tm], 1, 0))
            last = jnp.where(row == S - 1, u[H + tm:H + tm + 1], pltpu.roll(mid[0:S], S - 1, 0))
            prev = jnp.concatenate([first, mid[0:tm - S]], axis=0)
            nxt = jnp.concatenate([mid[S:tm], last], axis=0)
            return cw[0:1] * prev + cw[1:2] * mid + cw[2:3] * nxt + cb
    else:
        pos = lax.broadcasted_iota(jnp.int32, (tm, 1), 0) % CTX_LEN
        has_prev = (pos != 0).astype(F32)
        has_next = (pos != CTX_LEN - 1).astype(F32)

        def conv(u, cw, cb):
            return (cw[0:1] * (pltpu.roll(u, 1, 0)[H:H + tm] * has_prev)
                    + cw[1:2] * u[H:H + tm]
                    + cw[2:3] * (pltpu.roll(u, N - 1, 0)[H:H + tm] * has_next)
                    + cb)

    hh = h_ref[...]
    ua = conv(jnp.dot(hh, wa_ref[...].astype(BF16), preferred_element_type=F32), cwa_ref[...], cba_ref[...])
    uv = conv(jnp.dot(hh, wv_ref[...].astype(BF16), preferred_element_type=F32), cwv_ref[...], cbv_ref[...])
    o_ref[...] = (_gelu_tanh(ua) * uv).astype(o_ref.dtype)


def _ffn_up_call(h_all, w_up, l, cw_l, cb2, *, banded, tile0, n_tiles, n_rows, out_buf):
    tm, tn = FFN_TM, 512
    nn = D_FF // tn
    per = tm // FFN_HALO
    last = T_ALL // FFN_HALO - 1
    in_specs = [
        pl.BlockSpec((FFN_HALO, D_MODEL), lambda i, n: (jnp.maximum((i + tile0) * per - 1, 0), 0)),
        pl.BlockSpec((tm, D_MODEL), lambda i, n: (i + tile0, 0)),
        pl.BlockSpec((FFN_HALO, D_MODEL), lambda i, n: (jnp.minimum((i + tile0 + 1) * per, last), 0)),
        pl.BlockSpec((None, D_MODEL, tn), lambda i, n: (l, 0, n)),
        pl.BlockSpec((None, D_MODEL, tn), lambda i, n: (l, 0, nn + n)),
        pl.BlockSpec((3, tn), lambda i, n: (0, n)),
        pl.BlockSpec((3, tn), lambda i, n: (0, nn + n)),
        pl.BlockSpec((1, tn), lambda i, n: (0, n)),
        pl.BlockSpec((1, tn), lambda i, n: (0, nn + n)),
    ]
    args = [h_all, h_all, h_all, w_up, w_up, cw_l, cw_l, cb2, cb2]
    aliases = {}
    if out_buf is not None:
        in_specs.append(pl.BlockSpec(memory_space=pl.ANY))
        args.append(out_buf)
        aliases = {len(args) - 1: 0}
    return pl.pallas_call(
        functools.partial(_ffn_up_kernel, banded=banded, tile0=tile0),
        grid=(n_tiles, nn),
        in_specs=in_specs,
        out_specs=pl.BlockSpec((tm, tn), lambda i, n: (i + tile0, n)),
        out_shape=jax.ShapeDtypeStruct((n_rows, D_FF), BF16),
        scratch_shapes=[pltpu.VMEM((tm + 2 * FFN_HALO, D_MODEL), BF16)],
        input_output_aliases=aliases,
        compiler_params=_params("parallel", "arbitrary"),
        name="ffn_up_%s" % ("latent" if banded else "context"),
    )(*args)


def _ffn_up(h_all, w_up, l, cw_l, cb_l, n_rows):
    cb2 = cb_l.reshape(1, 2 * D_FF)
    act = _ffn_up_call(h_all, w_up, l, cw_l, cb2, banded=True, tile0=0,
                       n_tiles=T_LAT // FFN_TM, n_rows=n_rows, out_buf=None)
    if n_rows > T_LAT:
        act = _ffn_up_call(h_all, w_up, l, cw_l, cb2, banded=False,
                           tile0=T_LAT // FFN_TM, n_tiles=T_CTX // FFN_TM, n_rows=n_rows, out_buf=act)
    return act


def _ffn_down_kernel(a_ref, w_ref, x_ref, mod_ref, g_ref, o_ref):
    f = jnp.dot(a_ref[...], w_ref[...], preferred_element_type=F32)
    y = f * _rms_scale(f) * g_ref[3:4, :]
    o_ref[...] = x_ref[...] + mod_ref[5:6, :] * y


def _ffn_down(act, w_down, l, x_all, mod_l, norm_g_l, n_rows, in_place):
    tm = 256
    return pl.pallas_call(
        _ffn_down_kernel,
        grid=(n_rows // tm,),
        in_specs=[
            pl.BlockSpec((tm, D_FF), lambda i: (i, 0)),
            pl.BlockSpec((None, D_FF, D_MODEL), lambda i: (l, 0, 0), pipeline_mode=pl.Buffered(1)),
            pl.BlockSpec((tm, D_MODEL), lambda i: (i, 0)),
            pl.BlockSpec((None, 6, D_MODEL), lambda i: (_mod_row(i, tm), 0, 0)),
            pl.BlockSpec((4, D_MODEL), lambda i: (0, 0)),
        ],
        out_specs=pl.BlockSpec((tm, D_MODEL), lambda i: (i, 0)),
        out_shape=jax.ShapeDtypeStruct((T_ALL if in_place else n_rows, D_MODEL), F32),
        input_output_aliases={2: 0} if in_place else {},
        compiler_params=_params("parallel"),
        name="ffn_down",
    )(act, w_down, x_all, mod_l, norm_g_l)


def kernel(x, c, ctx, c_ctx, w_mod, b_mod, norm_g, w_in, hg_lower_bounds, hg_norm_g, rg_conv_w,
           rg_conv_b, rg_wa, rg_ba, rg_wx, rg_bx, rg_lambda, w_out, ffn_w_up, ffn_conv_w,
           ffn_conv_b, ffn_w_down):
    x_all = jnp.concatenate([_to_banded(x), ctx.reshape(T_CTX, D_MODEL)], axis=0)
    cc = jnp.concatenate([c, c_ctx[None, :], jnp.zeros((MOD_ROWS - BATCH - 1, D_MODEL), F32)], axis=0)
    mods = _modulation(cc, w_mod, b_mod)
    lb_all = _lower_bounds(hg_lower_bounds)

    w_in, w_out, ffn_w_down = (w.astype(BF16) for w in (w_in, w_out, ffn_w_down))

    for l in range(DEPTH):
        ctx_out = l < DEPTH - 1
        n_rows = T_ALL if ctx_out else T_LAT
        p_all = _inproj(x_all, mods[l], norm_g[l], lb_all[l], w_in, l)
        o_f, o_b = _hgrn_mixer(p_all)
        rg = _rg_mixer(p_all, rg_conv_w[l], rg_conv_b[l], rg_wa[l], rg_ba[l], rg_wx[l], rg_bx[l],
                       rg_lambda[l])
        x_all, h_all = _outproj(o_f, o_b, p_all, hg_norm_g[l].reshape(1, HG_WIDTH), rg, w_out, l, x_all,
                                mods[l], norm_g[l], n_rows)
        act = _ffn_up(h_all, ffn_w_up, l, ffn_conv_w[l], ffn_conv_b[l], n_rows)
        x_all = _ffn_down(act, ffn_w_down, l, x_all, mods[l], norm_g[l], n_rows, in_place=ctx_out)
    return _from_banded(x_all)
```

```python
import functools
import math

import jax
import jax.numpy as jnp
from jax import lax
from jax.experimental import pallas as pl
from jax.experimental.pallas import tpu as pltpu

F32 = jnp.float32
BF16 = jnp.bfloat16

D_MODEL = 2048
BATCH = 4
SEQ = 4096
DEPTH = 4
GRID_W = 64
GRID_ROWS = SEQ // GRID_W
CTX_LEN = 256
HG_WIDTH = 1024
HG_HEADS = 8
HG_D = HG_WIDTH // HG_HEADS
CHUNK = 64
RG_WIDTH = 1024
RG_BLOCKS = 8
RG_BW = RG_WIDTH // RG_BLOCKS
RG_C = 8.0
D_FF = 5632
N_PROJ = 5 * HG_WIDTH + 2 * RG_WIDTH
EPS = 1e-6

T_LAT = BATCH * SEQ
T_CTX = BATCH * CTX_LEN
T_ALL = T_LAT + T_CTX
MOD_ROWS = 8
SUBLANES = 8
SLAB = 16
BAND = SLAB * GRID_W
BANDS = SEQ // BAND
EXP_CLAMP = 80.0
HG_SEQS = 4
OUTPROJ_SUBTILES = 2

VMEM_LIMIT = 56 * 1024 * 1024

COL_Q, COL_FF, COL_FB, COL_V, COL_G, COL_RX, COL_RGATE = range(7)

NT_DIMS = (((1,), (1,)), ((), ()))
TN_DIMS = (((0,), (0,)), ((), ()))


def _sigmoid(x):
    return 0.5 + 0.5 * jnp.tanh(0.5 * x)


def _silu(x):
    return x * _sigmoid(x)


def _gelu_tanh(x):
    c = math.sqrt(2.0 / math.pi)
    return x * (0.5 * (1.0 + jnp.tanh(c * (x + 0.044715 * (x * x * x)))))


def _rms_scale(x):
    return lax.rsqrt(jnp.mean(x * x, axis=-1, keepdims=True) + EPS)


def _params(*sem):
    return pltpu.CompilerParams(dimension_semantics=sem, vmem_limit_bytes=VMEM_LIMIT)


def _mod_row(i, tm):
    return jnp.minimum(i // (SEQ // tm), BATCH)


def _to_banded(x):
    t = x.reshape(BATCH, BANDS, SLAB, GRID_W, D_MODEL)
    return t.transpose(0, 1, 3, 2, 4).reshape(T_LAT, D_MODEL)


def _from_banded(y):
    t = y.reshape(BATCH, BANDS, GRID_W, SLAB, D_MODEL)
    return t.transpose(0, 1, 3, 2, 4).reshape(BATCH, SEQ, D_MODEL)


def _lb_kernel(x_ref, o_ref):
    x = x_ref[...]
    m = jnp.max(x, axis=0, keepdims=True)
    e = jnp.exp(x - m)
    sm = e / jnp.sum(e, axis=0, keepdims=True)
    acc = jnp.zeros_like(sm[0:1])
    o_ref[0:1, :] = acc
    for l in range(1, DEPTH):
        acc = acc + sm[l:l + 1]
        o_ref[l:l + 1, :] = acc


def _lower_bounds(hg_lower_bounds):
    x = hg_lower_bounds.reshape(DEPTH, 2 * HG_WIDTH)
    out = pl.pallas_call(
        _lb_kernel,
        out_shape=jax.ShapeDtypeStruct((DEPTH, 2 * HG_WIDTH), F32),
        name="hg_lower_bounds",
    )(x)
    return out.reshape(DEPTH, 2, HG_WIDTH)


def _mod_kernel(cc_ref, w_ref, b_ref, o_ref):
    k = pl.program_id(1)

    @pl.when(k == 0)
    def _():
        o_ref[...] = jnp.broadcast_to(b_ref[...], o_ref.shape)

    s = _silu(cc_ref[...])
    o_ref[...] += jnp.dot(s.astype(BF16), w_ref[...].astype(BF16), preferred_element_type=F32)


def _modulation(cc, w_mod, b_mod):
    tk = 256
    n_out = 6 * D_MODEL
    out = pl.pallas_call(
        _mod_kernel,
        grid=(DEPTH, D_MODEL // tk),
        in_specs=[
            pl.BlockSpec((MOD_ROWS, tk), lambda l, k: (0, k)),
            pl.BlockSpec((None, tk, n_out), lambda l, k: (l, k, 0)),
            pl.BlockSpec((None, 1, n_out), lambda l, k: (l, 0, 0)),
        ],
        out_specs=pl.BlockSpec((None, MOD_ROWS, n_out), lambda l, k: (l, 0, 0)),
        out_shape=jax.ShapeDtypeStruct((DEPTH, MOD_ROWS, n_out), F32),
        compiler_params=_params("parallel", "arbitrary"),
        name="adaln_modulation",
    )(cc, w_mod, b_mod.reshape(DEPTH, 1, n_out))
    return out.reshape(DEPTH, MOD_ROWS, 6, D_MODEL)


def _inproj_kernel(x_ref, mod_ref, g_ref, lb_ref, w_ref, o_ref):
    x = x_ref[...]
    y = x * _rms_scale(x) * g_ref[0:1, :]
    h = (y * (1.0 + mod_ref[1:2, :]) + mod_ref[0:1, :]).astype(BF16)

    def gate(lb):
        return lambda z: lb + (1.0 - lb) * _sigmoid(z)

    acts = {COL_Q: _silu, COL_FF: gate(lb_ref[0:1, :]), COL_FB: gate(lb_ref[1:2, :]), COL_V: None,
            COL_G: _silu, COL_RX: None, COL_RGATE: _gelu_tanh}
    for col, act in acts.items():
        cs = slice(col * HG_WIDTH, (col + 1) * HG_WIDTH)
        z = jnp.dot(h, w_ref[:, cs], preferred_element_type=F32)
        o_ref[:, cs] = z if act is None else act(z)


def _inproj(x_all, mod_l, norm_g_l, lb_l, w_in, l):
    tm = 256
    return pl.pallas_call(
        _inproj_kernel,
        grid=(T_ALL // tm,),
        in_specs=[
            pl.BlockSpec((tm, D_MODEL), lambda i: (i, 0)),
            pl.BlockSpec((None, 6, D_MODEL), lambda i: (_mod_row(i, tm), 0, 0)),
            pl.BlockSpec((4, D_MODEL), lambda i: (0, 0)),
            pl.BlockSpec((2, HG_WIDTH), lambda i: (0, 0)),
            pl.BlockSpec((None, D_MODEL, N_PROJ), lambda i: (l, 0, 0), pipeline_mode=pl.Buffered(1)),
        ],
        out_specs=pl.BlockSpec((tm, N_PROJ), lambda i: (i, 0)),
        out_shape=jax.ShapeDtypeStruct((T_ALL, N_PROJ), F32),
        compiler_params=_params("parallel"),
        name="in_projection",
    )(x_all, mod_l, norm_g_l, lb_l, w_in)


def _hgrn_chunk(q, f, v, states, rev):
    C = CHUNK
    NB = C // 16
    vb = v.astype(BF16)
    k = 1.0 - f
    g = jnp.log(f)

    def scan_pos(idx):
        return (C - 1 - idx) if rev else idx

    pr = scan_pos(lax.broadcasted_iota(jnp.int32, (C, C), 0))
    pc = scan_pos(lax.broadcasted_iota(jnp.int32, (C, C), 1))
    tri = (pc <= pr).astype(BF16)
    g_hi = g.astype(BF16)
    r1 = g - g_hi.astype(F32)
    g_mid = r1.astype(BF16)
    g_lo = (r1 - g_mid.astype(F32)).astype(BF16)
    b = (jnp.dot(tri, g_hi, preferred_element_type=F32)
         + jnp.dot(tri, g_mid, preferred_element_type=F32)
         + jnp.dot(tri, g_lo, preferred_element_type=F32))

    def rows(p0, p1):
        return slice(C - p1, C - p0) if rev else slice(p0, p1)

    def brow(sp):
        r = scan_pos(sp)
        return b[r:r + 1, :]

    def assemble(blocks):
        zero = jnp.zeros((16, HG_WIDTH), F32)
        blocks = [zero if blk is None else blk for blk in blocks]
        if rev:
            blocks = blocks[::-1]
        return jnp.concatenate(blocks, axis=0).astype(BF16)

    q0b, k0b = [], []
    for j in range(NB):
        R = rows(16 * j, 16 * j + 16)
        d = jnp.clip(b[R] - 0.5 * (brow(16 * j) + brow(16 * j + 15)), -EXP_CLAMP, EXP_CLAMP)
        q0b.append(q[R] * jnp.exp(d))
        k0b.append(k[R] * jnp.exp(-d))
    q0, k0 = assemble(q0b), assemble(k0b)
    q1b, k1b = [], []
    for m in range(NB // 2):
        ref = brow(32 * m + 15)
        Re, Ro = rows(32 * m, 32 * m + 16), rows(32 * m + 16, 32 * m + 32)
        k1b += [k[Re] * jnp.exp(ref - b[Re]), None]
        q1b += [None, q[Ro] * jnp.exp(b[Ro] - ref)]
    q1, k1 = assemble(q1b), assemble(k1b)
    ref = brow(31)
    q2b = [None, None] + [q[rows(16 * j, 16 * j + 16)] * jnp.exp(b[rows(16 * j, 16 * j + 16)] - ref)
                          for j in (2, 3)]
    k2b = [k[rows(16 * j, 16 * j + 16)] * jnp.exp(ref - b[rows(16 * j, 16 * j + 16)])
           for j in (0, 1)] + [None, None]
    q2, k2 = assemble(q2b), assemble(k2b)
    b_last = brow(C - 1)
    q3 = (q * jnp.exp(b)).astype(BF16)
    k3 = (k * jnp.exp(b_last - b)).astype(BF16)
    dec = jnp.exp(b_last)

    m0 = ((pr >> 4) == (pc >> 4)) & (pc <= pr)
    m1 = ((pr >> 5) == (pc >> 5)) & (((pr >> 4) & 1) == 1) & (((pc >> 4) & 1) == 0)

    heads = [slice(h * HG_D, (h + 1) * HG_D) for h in range(HG_HEADS)]
    stage1 = []
    for h, sl in enumerate(heads):
        a0 = lax.dot_general(q0[:, sl], k0[:, sl], NT_DIMS, preferred_element_type=F32)
        a1 = lax.dot_general(q1[:, sl], k1[:, sl], NT_DIMS, preferred_element_type=F32)
        a2 = lax.dot_general(q2[:, sl], k2[:, sl], NT_DIMS, preferred_element_type=F32)
        inter = lax.dot_general(q3[:, sl], states[h].astype(BF16), NT_DIMS, preferred_element_type=F32)
        upd = lax.dot_general(vb[:, sl], k3[:, sl], TN_DIMS, preferred_element_type=F32)
        stage1.append((a0, a1, a2, inter, upd))
    outs, new_states = [], []
    for h, sl in enumerate(heads):
        a0, a1, a2, inter, upd = stage1[h]
        scores = jnp.where(m0, a0, jnp.where(m1, a1, a2))
        outs.append(jnp.dot(scores.astype(BF16), vb[:, sl], preferred_element_type=F32) + inter)
        new_states.append(states[h] * dec[:, sl] + upd)
    return jnp.concatenate(outs, axis=1), new_states


def _hgrn_kernel(qf_ref, ff_ref, vf_ref, qb_ref, fb_ref, vb_ref, s0_ref, *rest):
    of_ref, ob_ref, sfin_ref, s_scr = rest[-4:]
    i = pl.program_id(1)

    @pl.when(i == 0)
    def _():
        s_scr[...] = s0_ref[...]

    def chunk_rows(ref, s):
        return ref[...].reshape(HG_SEQS, CHUNK, HG_WIDTH)[s]

    results = {}
    for s in range(HG_SEQS):
        for d, (q_ref, f_ref, v_ref) in enumerate(((qf_ref, ff_ref, vf_ref), (qb_ref, fb_ref, vb_ref))):
            states = [s_scr[s, d, h] for h in range(HG_HEADS)]
            results[s, d] = _hgrn_chunk(chunk_rows(q_ref, s), chunk_rows(f_ref, s), chunk_rows(v_ref, s),
                                        states, rev=(d == 1))
    for d, o_ref in enumerate((of_ref, ob_ref)):
        for s in range(HG_SEQS):
            for h in range(HG_HEADS):
                s_scr[s, d, h] = results[s, d][1][h]
        o_ref[...] = jnp.stack([results[s, d][0] for s in range(HG_SEQS)]).reshape(o_ref.shape)

    @pl.when(i == pl.num_programs(1) - 1)
    def _():
        sfin_ref[...] = s_scr[...]


def _hgrn_call(p_all, s0, *, latent, out_bufs=None):
    n_chunks = (SEQ if latent else CTX_LEN) // CHUNK
    G = HG_SEQS

    if latent:
        def view(a):
            return a.reshape(T_ALL // BAND, BAND, a.shape[-1])

        def spec(col, rev):
            return pl.BlockSpec((G * BANDS, SLAB, HG_WIDTH),
                                lambda b, i: (b, (n_chunks - 1 - i) if rev else i, col))
    else:
        def view(a):
            return a.reshape(T_ALL // CTX_LEN, CTX_LEN, a.shape[-1])

        def spec(col, rev):
            return pl.BlockSpec((G, CHUNK, HG_WIDTH),
                                lambda b, i: (T_LAT // CTX_LEN // G + b, (n_chunks - 1 - i) if rev else i, col))

    p_view = view(p_all)
    o_shape = jax.ShapeDtypeStruct(p_view.shape[:-1] + (HG_WIDTH,), F32)
    state_spec = pl.BlockSpec((G, 2, HG_HEADS, HG_D, HG_D), lambda b, i: (b, 0, 0, 0, 0))
    in_specs = [spec(COL_Q, False), spec(COL_FF, False), spec(COL_V, False),
                spec(COL_Q, True), spec(COL_FB, True), spec(COL_V, True), state_spec]
    args = [p_view] * 6 + [s0]
    aliases = {}
    if out_bufs is not None:
        in_specs += [pl.BlockSpec(memory_space=pl.ANY)] * 2
        args += [view(buf) for buf in out_bufs]
        aliases = {len(args) - 2: 0, len(args) - 1: 1}

    o_f, o_b, s_fin = pl.pallas_call(
        _hgrn_kernel,
        grid=(BATCH // G, n_chunks),
        in_specs=in_specs,
        out_specs=[spec(0, False), spec(0, True), state_spec],
        out_shape=[o_shape, o_shape, jax.ShapeDtypeStruct((BATCH, 2, HG_HEADS, HG_D, HG_D), F32)],
        scratch_shapes=[pltpu.VMEM((G, 2, HG_HEADS, HG_D, HG_D), F32)],
        input_output_aliases=aliases,
        compiler_params=_params("parallel", "arbitrary"),
        name="hgrn2_%s" % ("latent" if latent else "context"),
    )(*args)
    return o_f.reshape(T_ALL, HG_WIDTH), o_b.reshape(T_ALL, HG_WIDTH), s_fin


def _hgrn_mixer(p_all):
    zero_s = jnp.zeros((BATCH, 2, HG_HEADS, HG_D, HG_D), F32)
    o_f, o_b, s_ctx = _hgrn_call(p_all, zero_s, latent=False)
    o_f, o_b, _ = _hgrn_call(p_all, s_ctx, latent=True, out_bufs=(o_f, o_b))
    return o_f, o_b


def _rg_gates(xc, wa_ref, ba_ref, wx_ref, bx_ref, lam_ref):
    xcb = xc.astype(BF16)
    r_parts, i_parts = [], []
    for n in range(RG_BLOCKS):
        sl = slice(n * RG_BW, (n + 1) * RG_BW)
        r_parts.append(jnp.dot(xcb[:, sl], wa_ref[n], preferred_element_type=F32))
        i_parts.append(jnp.dot(xcb[:, sl], wx_ref[n], preferred_element_type=F32))
    r = _sigmoid(jnp.concatenate(r_parts, axis=1) + ba_ref[...])
    ig = _sigmoid(jnp.concatenate(i_parts, axis=1) + bx_ref[...])
    nl = -lam_ref[...]
    softplus = jnp.maximum(nl, 0.0) + jnp.log1p(jnp.exp(-jnp.abs(nl)))
    log_a = (-RG_C) * r * softplus
    a = jnp.exp(log_a)
    return a, jnp.sqrt(-jnp.tanh(log_a) * (1.0 + a * a)) * ig * xc


def _rg_ctx_kernel(*refs, rev, final):
    x_ref, cw_ref, cb_ref, wa_ref, ba_ref, wx_ref, bx_ref, lam_ref = refs[:8]
    if final:
        hb_ref, gate_ref, o_ref, hfin_ref = refs[8:]
    else:
        o_ref, hfin_ref = refs[8:]
    R = CTX_LEN
    N = R + 2 * SUBLANES
    x = x_ref[...]
    pad = jnp.zeros((SUBLANES, RG_WIDTH), F32)
    ext = jnp.concatenate([pad, x, pad], axis=0)
    lo, hi_ = SUBLANES, SUBLANES + R
    xc = (cw_ref[0:1, :] * pltpu.roll(ext, 2, 0)[lo:hi_]
          + cw_ref[1:2, :] * pltpu.roll(ext, 1, 0)[lo:hi_]
          + cw_ref[2:3, :] * x
          + cw_ref[3:4, :] * pltpu.roll(ext, N - 1, 0)[lo:hi_]
          + cb_ref[...])
    a, bt = _rg_gates(xc, wa_ref, ba_ref, wx_ref, bx_ref, lam_ref)

    row = lax.broadcasted_iota(jnp.int32, (R, RG_WIDTH), 0)
    step = 1
    while step < R:
        if rev:
            keep = row < R - step
            shift = R - step
        else:
            keep = row >= step
            shift = step
        a_sh = jnp.where(keep, pltpu.roll(a, shift, 0), 1.0)
        b_sh = jnp.where(keep, pltpu.roll(bt, shift, 0), 0.0)
        bt = bt + a * b_sh
        a = a * a_sh
        step *= 2
    h = bt
    end = 0 if rev else R - 1
    hfin_ref[...] = h[end:end + 1, :]
    if final:
        o_ref[...] = ((h + hb_ref[...]) * gate_ref[...]).astype(o_ref.dtype)
    else:
        o_ref[...] = h


def _rg_lat_kernel(*refs, rev, final):
    xp_ref, x_ref, xn_ref, cw_ref, cb_ref, wa_ref, ba_ref, wx_ref, bx_ref, lam_ref, h0_ref = refs[:11]
    if final:
        hb_ref, gate_ref, _alias, o_ref, ext, a_scr, h_scr, carry = refs[11:]
    else:
        _alias, o_ref, ext, a_scr, h_scr, carry = refs[11:]
    S = SLAB
    j = pl.program_id(1)
    band = (BANDS - 1 - j) if rev else j

    @pl.when(j == 0)
    def _():
        carry[...] = h0_ref[...]

    row = lax.broadcasted_iota(jnp.int32, (S, RG_WIDTH), 0)
    has_prev = (band > 0).astype(F32)
    has_next = (band < BANDS - 1).astype(F32)

    def shift_down(slab, top):
        return jnp.where(row == 0, top, pltpu.roll(slab, 1, 0))

    def shift_up(slab, bottom):
        return jnp.where(row == S - 1, bottom, pltpu.roll(slab, S - 1, 0))

    ext[0:S, :] = shift_down(x_ref[BAND - 2 * S:BAND - S, :], xp_ref[S - 1:S, :] * has_prev)
    ext[S:2 * S, :] = shift_down(x_ref[BAND - S:BAND, :], xp_ref[2 * S - 1:2 * S, :] * has_prev)
    ext[2 * S:2 * S + BAND, :] = x_ref[...]
    ext[2 * S + BAND:3 * S + BAND, :] = shift_up(x_ref[0:S, :], xn_ref[0:1, :] * has_next)

    CH = 256
    for s in range(0, BAND, CH):
        xc = (cw_ref[0:1, :] * ext[s:s + CH, :]
              + cw_ref[1:2, :] * ext[s + S:s + S + CH, :]
              + cw_ref[2:3, :] * ext[s + 2 * S:s + 2 * S + CH, :]
              + cw_ref[3:4, :] * ext[s + 3 * S:s + 3 * S + CH, :]
              + cb_ref[...])
        a, bt = _rg_gates(xc, wa_ref, ba_ref, wx_ref, bx_ref, lam_ref)
        a_scr[s:s + CH, :] = a
        h_scr[s:s + CH, :] = bt

    def body(t, hc):
        h, ac = hc
        c = (GRID_W - 1 - t) if rev else t
        sl = pl.ds(pl.multiple_of(c * S, S), S)
        a_c = a_scr[sl, :]
        h = a_c * h + h_scr[sl, :]
        ac = a_c * ac
        h_scr[sl, :] = h
        a_scr[sl, :] = ac
        return h, ac

    h_end, a_end = lax.fori_loop(0, GRID_W, body,
                                 (jnp.zeros((S, RG_WIDTH), F32), jnp.ones((S, RG_WIDTH), F32)),
                                 unroll=4)
    cin = jnp.zeros((S, RG_WIDTH), F32)
    c_run = carry[...]
    order = range(S - 1, -1, -1) if rev else range(S)
    for r in order:
        cin = jnp.where(row == r, c_run, cin)
        c_run = h_end[r:r + 1, :] + a_end[r:r + 1, :] * c_run
    carry[...] = c_run
    h = (h_scr[...].reshape(GRID_W, S, RG_WIDTH)
         + a_scr[...].reshape(GRID_W, S, RG_WIDTH) * cin[None]).reshape(BAND, RG_WIDTH)
    if final:
        o_ref[...] = ((h + hb_ref[...]) * gate_ref[...]).astype(o_ref.dtype)
    else:
        o_ref[...] = h


def _rg_call(p_all, w, *, rev, h_back=None):
    cw, cb, wa, ba, wx, bx, lam = w
    final = h_back is not None
    out_dtype = BF16 if final else F32

    def const(shape):
        return pl.BlockSpec(shape, lambda *_: (0,) * len(shape))

    w_specs = [const((4, RG_WIDTH)), const((1, RG_WIDTH)),
               const((RG_BLOCKS, RG_BW, RG_BW)), const((1, RG_WIDTH)),
               const((RG_BLOCKS, RG_BW, RG_BW)), const((1, RG_WIDTH)), const((1, RG_WIDTH))]
    w_args = [cw, cb, wa, ba, wx, bx, lam]
    carry_spec = pl.BlockSpec((None, 1, RG_WIDTH), lambda b, *_: (b, 0, 0))
    carry_shape = jax.ShapeDtypeStruct((BATCH, 1, RG_WIDTH), F32)

    def cmain(col):
        return pl.BlockSpec((CTX_LEN, RG_WIDTH), lambda b: (T_LAT // CTX_LEN + b, col))

    in_specs = [cmain(COL_RX)] + w_specs
    args = [p_all] + w_args
    if final:
        in_specs += [cmain(0), cmain(COL_RGATE)]
        args += [h_back, p_all]
    out_c, h_ctx = pl.pallas_call(
        functools.partial(_rg_ctx_kernel, rev=rev, final=final),
        grid=(BATCH,),
        in_specs=in_specs,
        out_specs=[cmain(0), carry_spec],
        out_shape=[jax.ShapeDtypeStruct((T_ALL, RG_WIDTH), out_dtype), carry_shape],
        compiler_params=_params("parallel"),
        name="rglru_context_%s" % ("bwd" if rev else "fwd"),
    )(*args)

    def band_idx(b, j):
        return b * BANDS + ((BANDS - 1 - j) if rev else j)

    def lmain(col):
        return pl.BlockSpec((BAND, RG_WIDTH), lambda b, j: (band_idx(b, j), col))

    per = BAND // (2 * SLAB)
    prev_spec = pl.BlockSpec((2 * SLAB, RG_WIDTH),
                             lambda b, j: (jnp.maximum(band_idx(b, j) * per - 1, 0), COL_RX))
    next_spec = pl.BlockSpec((SLAB, RG_WIDTH),
                             lambda b, j: ((band_idx(b, j) + 1) * (BAND // SLAB), COL_RX))
    in_specs = [prev_spec, lmain(COL_RX), next_spec] + w_specs + [carry_spec]
    args = [p_all, p_all, p_all] + w_args + [h_ctx]
    if final:
        in_specs += [lmain(0), lmain(COL_RGATE)]
        args += [h_back, p_all]
    in_specs.append(pl.BlockSpec(memory_space=pl.ANY))
    args.append(out_c)
    return pl.pallas_call(
        functools.partial(_rg_lat_kernel, rev=rev, final=final),
        grid=(BATCH, BANDS),
        in_specs=in_specs,
        out_specs=lmain(0),
        out_shape=jax.ShapeDtypeStruct((T_ALL, RG_WIDTH), out_dtype),
        scratch_shapes=[pltpu.VMEM((BAND + 3 * SLAB, RG_WIDTH), F32),
                        pltpu.VMEM((BAND, RG_WIDTH), F32),
                        pltpu.VMEM((BAND, RG_WIDTH), F32),
                        pltpu.VMEM((1, RG_WIDTH), F32)],
        input_output_aliases={len(args) - 1: 0},
        compiler_params=_params("parallel", "arbitrary"),
        name="rglru_latent_%s" % ("bwd" if rev else "fwd"),
    )(*args)


def _rg_mixer(p_all, cw, cb, wa, ba, wx, bx, lam):
    row = lambda t: t.reshape(1, RG_WIDTH)

    def weights(d):
        return (cw, row(cb), wa[d].astype(BF16), row(ba[d]), wx[d].astype(BF16), row(bx[d]), row(lam[d]))

    h_b = _rg_call(p_all, weights(1), rev=True)
    return _rg_call(p_all, weights(0), rev=False, h_back=h_b)


def _outproj_kernel(of_ref, ob_ref, sg_ref, hgg_ref, rg_ref, w_ref, x_ref, mod_ref, g_ref, o_ref,
                    h_ref):
    sub = o_ref.shape[0] // OUTPROJ_SUBTILES
    for t in range(OUTPROJ_SUBTILES):
        rs = slice(t * sub, (t + 1) * sub)
        o = of_ref[rs, :] + ob_ref[rs, :]
        gain = hgg_ref[...] * sg_ref[rs, :]
        parts = []
        for h in range(HG_HEADS):
            sl = slice(h * HG_D, (h + 1) * HG_D)
            o_h = o[:, sl]
            parts.append((o_h * _rms_scale(o_h) * gain[:, sl]).astype(BF16))
        hg = jnp.concatenate(parts, axis=1)
        mix = (jnp.dot(hg, w_ref[0:HG_WIDTH, :], preferred_element_type=F32)
               + jnp.dot(rg_ref[rs, :], w_ref[HG_WIDTH:, :], preferred_element_type=F32))
        y = mix * _rms_scale(mix) * g_ref[1:2, :]
        x_new = x_ref[rs, :] + mod_ref[2:3, :] * y
        o_ref[rs, :] = x_new
        hn = x_new * _rms_scale(x_new) * g_ref[2:3, :]
        h_ref[rs, :] = (hn * (1.0 + mod_ref[4:5, :]) + mod_ref[3:4, :]).astype(BF16)


def _outproj(o_f, o_b, p_all, hgg, rg, w_out, l, x_all, mod_l, norm_g_l, n_rows):
    tm = 512
    row = lambda col: pl.BlockSpec((tm, HG_WIDTH), lambda i: (i, col))
    return pl.pallas_call(
        _outproj_kernel,
        grid=(n_rows // tm,),
        in_specs=[
            row(0), row(0), row(COL_G),
            pl.BlockSpec((1, HG_WIDTH), lambda i: (0, 0)),
            row(0),
            pl.BlockSpec((None, D_MODEL, D_MODEL), lambda i: (l, 0, 0), pipeline_mode=pl.Buffered(1)),
            pl.BlockSpec((tm, D_MODEL), lambda i: (i, 0)),
            pl.BlockSpec((None, 6, D_MODEL), lambda i: (_mod_row(i, tm), 0, 0)),
            pl.BlockSpec((4, D_MODEL), lambda i: (0, 0)),
        ],
        out_specs=[pl.BlockSpec((tm, D_MODEL), lambda i: (i, 0)),
                   pl.BlockSpec((tm, D_MODEL), lambda i: (i, 0))],
        out_shape=[jax.ShapeDtypeStruct((T_ALL, D_MODEL), F32),
                   jax.ShapeDtypeStruct((T_ALL, D_MODEL), BF16)],
        input_output_aliases={6: 0},
        compiler_params=_params("parallel"),
        name="out_projection",
    )(o_f, o_b, p_all, hgg, rg, w_out, x_all, mod_l, norm_g_l)


FFN_TM = BAND
FFN_HALO = SLAB


def _ffn_up_kernel(*refs, banded, tile0):
    hp_ref, hm_ref, hn_ref, wa_ref, wv_ref, cwa_ref, cwv_ref, cba_ref, cbv_ref = refs[:9]
    o_ref, h_ref = refs[-2:]
    i = pl.program_id(0) + tile0
    n = pl.program_id(1)
    tm, H, S = FFN_TM, FFN_HALO, SLAB
    N = tm + 2 * H

    @pl.when(n == 0)
    def _():
        hp, hn = hp_ref[...], hn_ref[...]
        if banded:
            band = i % BANDS
            hp = jnp.where(band > 0, hp, jnp.zeros_like(hp))
            hn = jnp.where(band < BANDS - 1, hn, jnp.zeros_like(hn))
        h_ref[0:H, :] = hp
        h_ref[H:H + tm, :] = hm_ref[...]
        h_ref[H + tm:N, :] = hn

    if banded:
        row = lax.broadcasted_iota(jnp.int32, (S, o_ref.shape[1]), 0)

        def conv(u, cw, cb):
            mid = u[H:H + tm]
            first = jnp.where(row == 0, u[H - 1:H], pltpu.roll(mid[tm - S:tm], 1, 0))
            last = jnp.where(row == S - 1, u[H + tm:H + tm + 1], pltpu.roll(mid[0:S], S - 1, 0))
            prev = jnp.concatenate([first, mid[0:tm - S]], axis=0)
            nxt = jnp.concatenate([mid[S:tm], last], axis=0)
            return cw[0:1] * prev + cw[1:2] * mid + cw[2:3] * nxt + cb
    else:
        pos = lax.broadcasted_iota(jnp.int32, (tm, 1), 0) % CTX_LEN
        has_prev = (pos != 0).astype(F32)
        has_next = (pos != CTX_LEN - 1).astype(F32)

        def conv(u, cw, cb):
            return (cw[0:1] * (pltpu.roll(u, 1, 0)[H:H + tm] * has_prev)
                    + cw[1:2] * u[H:H + tm]
                    + cw[2:3] * (pltpu.roll(u, N - 1, 0)[H:H + tm] * has_next)
                    + cb)

    hh = h_ref[...]
    ua = conv(jnp.dot(hh, wa_ref[...].astype(BF16), preferred_element_type=F32), cwa_ref[...], cba_ref[...])
    uv = conv(jnp.dot(hh, wv_ref[...].astype(BF16), preferred_element_type=F32), cwv_ref[...], cbv_ref[...])
    o_ref[...] = (_gelu_tanh(ua) * uv).astype(o_ref.dtype)


def _ffn_up_call(h_all, w_up, l, cw_l, cb2, *, banded, tile0, n_tiles, n_rows, out_buf):
    tm, tn = FFN_TM, 512
    nn = D_FF // tn
    per = tm // FFN_HALO
    last = n_rows // FFN_HALO - 1
    in_specs = [
        pl.BlockSpec((FFN_HALO, D_MODEL), lambda i, n: (jnp.maximum((i + tile0) * per - 1, 0), 0)),
        pl.BlockSpec((tm, D_MODEL), lambda i, n: (i + tile0, 0)),
        pl.BlockSpec((FFN_HALO, D_MODEL), lambda i, n: (jnp.minimum((i + tile0 + 1) * per, last), 0)),
        pl.BlockSpec((None, D_MODEL, tn), lambda i, n: (l, 0, n)),
        pl.BlockSpec((None, D_MODEL, tn), lambda i, n: (l, 0, nn + n)),
        pl.BlockSpec((3, tn), lambda i, n: (0, n)),
        pl.BlockSpec((3, tn), lambda i, n: (0, nn + n)),
        pl.BlockSpec((1, tn), lambda i, n: (0, n)),
        pl.BlockSpec((1, tn), lambda i, n: (0, nn + n)),
    ]
    args = [h_all, h_all, h_all, w_up, w_up, cw_l, cw_l, cb2, cb2]
    aliases = {}
    if out_buf is not None:
        in_specs.append(pl.BlockSpec(memory_space=pl.ANY))
        args.append(out_buf)
        aliases = {len(args) - 1: 0}
    return pl.pallas_call(
        functools.partial(_ffn_up_kernel, banded=banded, tile0=tile0),
        grid=(n_tiles, nn),
        in_specs=in_specs,
        out_specs=pl.BlockSpec((tm, tn), lambda i, n: (i + tile0, n)),
        out_shape=jax.ShapeDtypeStruct((n_rows, D_FF), BF16),
        scratch_shapes=[pltpu.VMEM((tm + 2 * FFN_HALO, D_MODEL), BF16)],
        input_output_aliases=aliases,
        compiler_params=_params("parallel", "arbitrary"),
        name="ffn_up_%s" % ("latent" if banded else "context"),
    )(*args)


def _ffn_up(h_all, w_up, l, cw_l, cb_l, n_rows):
    cb2 = cb_l.reshape(1, 2 * D_FF)
    act = _ffn_up_call(h_all, w_up, l, cw_l, cb2, banded=True, tile0=0,
                       n_tiles=T_LAT // FFN_TM, n_rows=n_rows, out_buf=None)
    if n_rows > T_LAT:
        act = _ffn_up_call(h_all, w_up, l, cw_l, cb2, banded=False,
                           tile0=T_LAT // FFN_TM, n_tiles=T_CTX // FFN_TM, n_rows=n_rows, out_buf=act)
    return act


def _ffn_down_kernel(a_ref, w_ref, x_ref, mod_ref, g_ref, o_ref):
    f = jnp.dot(a_ref[...], w_ref[...], preferred_element_type=F32)
    y = f * _rms_scale(f) * g_ref[3:4, :]
    o_ref[...] = x_ref[...] + mod_ref[5:6, :] * y


def _ffn_down(act, w_down, l, x_all, mod_l, norm_g_l, n_rows, in_place):
    tm = 512
    return pl.pallas_call(
        _ffn_down_kernel,
        grid=(n_rows // tm,),
        in_specs=[
            pl.BlockSpec((tm, D_FF), lambda i: (i, 0)),
            pl.BlockSpec((None, D_FF, D_MODEL), lambda i: (l, 0, 0), pipeline_mode=pl.Buffered(1)),
            pl.BlockSpec((tm, D_MODEL), lambda i: (i, 0)),
            pl.BlockSpec((None, 6, D_MODEL), lambda i: (_mod_row(i, tm), 0, 0)),
            pl.BlockSpec((4, D_MODEL), lambda i: (0, 0)),
        ],
        out_specs=pl.BlockSpec((tm, D_MODEL), lambda i: (i, 0)),
        out_shape=jax.ShapeDtypeStruct((T_ALL if in_place else n_rows, D_MODEL), F32),
        input_output_aliases={2: 0} if in_place else {},
        compiler_params=_params("parallel"),
        name="ffn_down",
    )(act, w_down, x_all, mod_l, norm_g_l)


def kernel(x, c, ctx, c_ctx, w_mod, b_mod, norm_g, w_in, hg_lower_bounds, hg_norm_g, rg_conv_w,
           rg_conv_b, rg_wa, rg_ba, rg_wx, rg_bx, rg_lambda, w_out, ffn_w_up, ffn_conv_w,
           ffn_conv_b, ffn_w_down):
    x_all = jnp.concatenate([_to_banded(x), ctx.reshape(T_CTX, D_MODEL)], axis=0)
    cc = jnp.concatenate([c, c_ctx[None, :], jnp.zeros((MOD_ROWS - BATCH - 1, D_MODEL), F32)], axis=0)
    mods = _modulation(cc, w_mod, b_mod)
    lb_all = _lower_bounds(hg_lower_bounds)

    w_in, w_out, ffn_w_down = (w.astype(BF16) for w in (w_in, w_out, ffn_w_down))

    for l in range(DEPTH):
        ctx_out = l < DEPTH - 1
        n_rows = T_ALL if ctx_out else T_LAT
        p_all = _inproj(x_all, mods[l], norm_g[l], lb_all[l], w_in, l)
        o_f, o_b = _hgrn_mixer(p_all)
        rg = _rg_mixer(p_all, rg_conv_w[l], rg_conv_b[l], rg_wa[l], rg_ba[l], rg_wx[l], rg_bx[l],
                       rg_lambda[l])
        x_all, h_all = _outproj(o_f, o_b, p_all, hg_norm_g[l].reshape(1, HG_WIDTH), rg, w_out, l, x_all,
                                mods[l], norm_g[l], n_rows)
        act = _ffn_up(h_all, ffn_w_up, l, ffn_conv_w[l], ffn_conv_b[l], n_rows)
        x_all = _ffn_down(act, ffn_w_down, l, x_all, mods[l], norm_g[l], n_rows, in_place=ctx_out)
    return _from_banded(x_all)
```

```python
import functools
import math

import jax
import jax.numpy as jnp
from jax import lax
from jax.experimental import pallas as pl
from jax.experimental.pallas import tpu as pltpu

F32 = jnp.float32
BF16 = jnp.bfloat16

D_MODEL = 2048
BATCH = 4
SEQ = 4096
DEPTH = 4
GRID_W = 64
GRID_ROWS = SEQ // GRID_W
CTX_LEN = 256
HG_WIDTH = 1024
HG_HEADS = 8
HG_D = HG_WIDTH // HG_HEADS
CHUNK = 64
RG_WIDTH = 1024
RG_BLOCKS = 8
RG_BW = RG_WIDTH // RG_BLOCKS
RG_C = 8.0
D_FF = 5632
N_PROJ = 5 * HG_WIDTH + 2 * RG_WIDTH
EPS = 1e-6

T_LAT = BATCH * SEQ
T_CTX = BATCH * CTX_LEN
T_ALL = T_LAT + T_CTX
MOD_ROWS = 8
SUBLANES = 8
SLAB = 16
BAND = SLAB * GRID_W
BANDS = SEQ // BAND
EXP_CLAMP = 80.0
HG_SEQS = 4
OUTPROJ_SUBTILES = 2

VMEM_LIMIT = 56 * 1024 * 1024

COL_Q, COL_FF, COL_FB, COL_V, COL_G, COL_RX, COL_RGATE = range(7)

NT_DIMS = (((1,), (1,)), ((), ()))
TN_DIMS = (((0,), (0,)), ((), ()))


def _sigmoid(x):
    return 0.5 + 0.5 * jnp.tanh(0.5 * x)


def _silu(x):
    return x * _sigmoid(x)


def _gelu_tanh(x):
    c = math.sqrt(2.0 / math.pi)
    return x * (0.5 * (1.0 + jnp.tanh(c * (x + 0.044715 * (x * x * x)))))


def _rms_scale(x):
    return lax.rsqrt(jnp.mean(x * x, axis=-1, keepdims=True) + EPS)


def _params(*sem):
    return pltpu.CompilerParams(dimension_semantics=sem, vmem_limit_bytes=VMEM_LIMIT)


def _mod_row(i, tm):
    return jnp.minimum(i // (SEQ // tm), BATCH)


def _to_banded(x):
    t = x.reshape(BATCH, BANDS, SLAB, GRID_W, D_MODEL)
    return t.transpose(0, 1, 3, 2, 4).reshape(T_LAT, D_MODEL)


def _from_banded(y):
    t = y.reshape(BATCH, BANDS, GRID_W, SLAB, D_MODEL)
    return t.transpose(0, 1, 3, 2, 4).reshape(BATCH, SEQ, D_MODEL)


def _lb_kernel(x_ref, o_ref):
    x = x_ref[...]
    m = jnp.max(x, axis=0, keepdims=True)
    e = jnp.exp(x - m)
    sm = e / jnp.sum(e, axis=0, keepdims=True)
    acc = jnp.zeros_like(sm[0:1])
    o_ref[0:1, :] = acc
    for l in range(1, DEPTH):
        acc = acc + sm[l:l + 1]
        o_ref[l:l + 1, :] = acc


def _lower_bounds(hg_lower_bounds):
    x = hg_lower_bounds.reshape(DEPTH, 2 * HG_WIDTH)
    out = pl.pallas_call(
        _lb_kernel,
        out_shape=jax.ShapeDtypeStruct((DEPTH, 2 * HG_WIDTH), F32),
        name="hg_lower_bounds",
    )(x)
    return out.reshape(DEPTH, 2, HG_WIDTH)


def _mod_kernel(cc_ref, w_ref, b_ref, o_ref):
    k = pl.program_id(1)

    @pl.when(k == 0)
    def _():
        o_ref[...] = jnp.broadcast_to(b_ref[...], o_ref.shape)

    s = _silu(cc_ref[...])
    o_ref[...] += jnp.dot(s.astype(BF16), w_ref[...].astype(BF16), preferred_element_type=F32)


def _modulation(cc, w_mod, b_mod):
    tk = 256
    n_out = 6 * D_MODEL
    out = pl.pallas_call(
        _mod_kernel,
        grid=(DEPTH, D_MODEL // tk),
        in_specs=[
            pl.BlockSpec((MOD_ROWS, tk), lambda l, k: (0, k)),
            pl.BlockSpec((None, tk, n_out), lambda l, k: (l, k, 0)),
            pl.BlockSpec((None, 1, n_out), lambda l, k: (l, 0, 0)),
        ],
        out_specs=pl.BlockSpec((None, MOD_ROWS, n_out), lambda l, k: (l, 0, 0)),
        out_shape=jax.ShapeDtypeStruct((DEPTH, MOD_ROWS, n_out), F32),
        compiler_params=_params("parallel", "arbitrary"),
        name="adaln_modulation",
    )(cc, w_mod, b_mod.reshape(DEPTH, 1, n_out))
    return out.reshape(DEPTH, MOD_ROWS, 6, D_MODEL)


def _inproj_kernel(x_ref, mod_ref, g_ref, lb_ref, w_ref, o_ref):
    x = x_ref[...]
    y = x * _rms_scale(x) * g_ref[0:1, :]
    h = (y * (1.0 + mod_ref[1:2, :]) + mod_ref[0:1, :]).astype(BF16)

    def gate(lb):
        return lambda z: lb + (1.0 - lb) * _sigmoid(z)

    acts = {COL_Q: _silu, COL_FF: gate(lb_ref[0:1, :]), COL_FB: gate(lb_ref[1:2, :]), COL_V: None,
            COL_G: _silu, COL_RX: None, COL_RGATE: _gelu_tanh}
    for col, act in acts.items():
        cs = slice(col * HG_WIDTH, (col + 1) * HG_WIDTH)
        z = jnp.dot(h, w_ref[:, cs], preferred_element_type=F32)
        o_ref[:, cs] = z if act is None else act(z)


def _inproj(x_all, mod_l, norm_g_l, lb_l, w_in, l):
    tm = 256
    return pl.pallas_call(
        _inproj_kernel,
        grid=(T_ALL // tm,),
        in_specs=[
            pl.BlockSpec((tm, D_MODEL), lambda i: (i, 0)),
            pl.BlockSpec((None, 6, D_MODEL), lambda i: (_mod_row(i, tm), 0, 0)),
            pl.BlockSpec((4, D_MODEL), lambda i: (0, 0)),
            pl.BlockSpec((2, HG_WIDTH), lambda i: (0, 0)),
            pl.BlockSpec((None, D_MODEL, N_PROJ), lambda i: (l, 0, 0), pipeline_mode=pl.Buffered(1)),
        ],
        out_specs=pl.BlockSpec((tm, N_PROJ), lambda i: (i, 0)),
        out_shape=jax.ShapeDtypeStruct((T_ALL, N_PROJ), F32),
        compiler_params=_params("parallel"),
        name="in_projection",
    )(x_all, mod_l, norm_g_l, lb_l, w_in)


def _hgrn_chunk(q, f, v, states, rev):
    C = CHUNK
    NB = C // 16
    vb = v.astype(BF16)
    k = 1.0 - f
    g = jnp.log(f)

    def scan_pos(idx):
        return (C - 1 - idx) if rev else idx

    pr = scan_pos(lax.broadcasted_iota(jnp.int32, (C, C), 0))
    pc = scan_pos(lax.broadcasted_iota(jnp.int32, (C, C), 1))
    tri = (pc <= pr).astype(BF16)
    g_hi = g.astype(BF16)
    r1 = g - g_hi.astype(F32)
    g_mid = r1.astype(BF16)
    g_lo = (r1 - g_mid.astype(F32)).astype(BF16)
    b = (jnp.dot(tri, g_hi, preferred_element_type=F32)
         + jnp.dot(tri, g_mid, preferred_element_type=F32)
         + jnp.dot(tri, g_lo, preferred_element_type=F32))

    def rows(p0, p1):
        return slice(C - p1, C - p0) if rev else slice(p0, p1)

    def brow(sp):
        r = scan_pos(sp)
        return b[r:r + 1, :]

    def assemble(blocks):
        zero = jnp.zeros((16, HG_WIDTH), F32)
        blocks = [zero if blk is None else blk for blk in blocks]
        if rev:
            blocks = blocks[::-1]
        return jnp.concatenate(blocks, axis=0).astype(BF16)

    q0b, k0b = [], []
    for j in range(NB):
        R = rows(16 * j, 16 * j + 16)
        d = jnp.clip(b[R] - 0.5 * (brow(16 * j) + brow(16 * j + 15)), -EXP_CLAMP, EXP_CLAMP)
        q0b.append(q[R] * jnp.exp(d))
        k0b.append(k[R] * jnp.exp(-d))
    q0, k0 = assemble(q0b), assemble(k0b)
    q1b, k1b = [], []
    for m in range(NB // 2):
        ref = brow(32 * m + 15)
        Re, Ro = rows(32 * m, 32 * m + 16), rows(32 * m + 16, 32 * m + 32)
        k1b += [k[Re] * jnp.exp(ref - b[Re]), None]
        q1b += [None, q[Ro] * jnp.exp(b[Ro] - ref)]
    q1, k1 = assemble(q1b), assemble(k1b)
    ref = brow(31)
    q2b = [None, None] + [q[rows(16 * j, 16 * j + 16)] * jnp.exp(b[rows(16 * j, 16 * j + 16)] - ref)
                          for j in (2, 3)]
    k2b = [k[rows(16 * j, 16 * j + 16)] * jnp.exp(ref - b[rows(16 * j, 16 * j + 16)])
           for j in (0, 1)] + [None, None]
    q2, k2 = assemble(q2b), assemble(k2b)
    b_last = brow(C - 1)
    q3 = (q * jnp.exp(b)).astype(BF16)
    k3 = (k * jnp.exp(b_last - b)).astype(BF16)
    dec = jnp.exp(b_last)

    m0 = ((pr >> 4) == (pc >> 4)) & (pc <= pr)
    m1 = ((pr >> 5) == (pc >> 5)) & (((pr >> 4) & 1) == 1) & (((pc >> 4) & 1) == 0)

    heads = [slice(h * HG_D, (h + 1) * HG_D) for h in range(HG_HEADS)]
    stage1 = []
    for h, sl in enumerate(heads):
        a0 = lax.dot_general(q0[:, sl], k0[:, sl], NT_DIMS, preferred_element_type=F32)
        a1 = lax.dot_general(q1[:, sl], k1[:, sl], NT_DIMS, preferred_element_type=F32)
        a2 = lax.dot_general(q2[:, sl], k2[:, sl], NT_DIMS, preferred_element_type=F32)
        inter = lax.dot_general(q3[:, sl], states[h].astype(BF16), NT_DIMS, preferred_element_type=F32)
        upd = lax.dot_general(vb[:, sl], k3[:, sl], TN_DIMS, preferred_element_type=F32)
        stage1.append((a0, a1, a2, inter, upd))
    outs, new_states = [], []
    for h, sl in enumerate(heads):
        a0, a1, a2, inter, upd = stage1[h]
        scores = jnp.where(m0, a0, jnp.where(m1, a1, a2))
        outs.append(jnp.dot(scores.astype(BF16), vb[:, sl], preferred_element_type=F32) + inter)
        new_states.append(states[h] * dec[:, sl] + upd)
    return jnp.concatenate(outs, axis=1), new_states


def _hgrn_kernel(qf_ref, ff_ref, vf_ref, qb_ref, fb_ref, vb_ref, s0_ref, *rest):
    of_ref, ob_ref, sfin_ref, s_scr = rest[-4:]
    i = pl.program_id(1)

    @pl.when(i == 0)
    def _():
        s_scr[...] = s0_ref[...]

    def chunk_rows(ref, s):
        return ref[...].reshape(HG_SEQS, CHUNK, HG_WIDTH)[s]

    results = {}
    for s in range(HG_SEQS):
        for d, (q_ref, f_ref, v_ref) in enumerate(((qf_ref, ff_ref, vf_ref), (qb_ref, fb_ref, vb_ref))):
            states = [s_scr[s, d, h] for h in range(HG_HEADS)]
            results[s, d] = _hgrn_chunk(chunk_rows(q_ref, s), chunk_rows(f_ref, s), chunk_rows(v_ref, s),
                                        states, rev=(d == 1))
    for d, o_ref in enumerate((of_ref, ob_ref)):
        for s in range(HG_SEQS):
            for h in range(HG_HEADS):
                s_scr[s, d, h] = results[s, d][1][h]
        o_ref[...] = jnp.stack([results[s, d][0] for s in range(HG_SEQS)]).reshape(o_ref.shape)

    @pl.when(i == pl.num_programs(1) - 1)
    def _():
        sfin_ref[...] = s_scr[...]


def _hgrn_call(p_all, s0, *, latent, out_bufs=None):
    n_chunks = (SEQ if latent else CTX_LEN) // CHUNK
    G = HG_SEQS

    if latent:
        def view(a):
            return a.reshape(T_ALL // BAND, BAND, a.shape[-1])

        def spec(col, rev):
            return pl.BlockSpec((G * BANDS, SLAB, HG_WIDTH),
                                lambda b, i: (b, (n_chunks - 1 - i) if rev else i, col))
    else:
        def view(a):
            return a.reshape(T_ALL // CTX_LEN, CTX_LEN, a.shape[-1])

        def spec(col, rev):
            return pl.BlockSpec((G, CHUNK, HG_WIDTH),
                                lambda b, i: (T_LAT // CTX_LEN // G + b, (n_chunks - 1 - i) if rev else i, col))

    p_view = view(p_all)
    o_shape = jax.ShapeDtypeStruct(p_view.shape[:-1] + (HG_WIDTH,), F32)
    state_spec = pl.BlockSpec((G, 2, HG_HEADS, HG_D, HG_D), lambda b, i: (b, 0, 0, 0, 0))
    in_specs = [spec(COL_Q, False), spec(COL_FF, False), spec(COL_V, False),
                spec(COL_Q, True), spec(COL_FB, True), spec(COL_V, True), state_spec]
    args = [p_view] * 6 + [s0]
    aliases = {}
    if out_bufs is not None:
        in_specs += [pl.BlockSpec(memory_space=pl.ANY)] * 2
        args += [view(buf) for buf in out_bufs]
        aliases = {len(args) - 2: 0, len(args) - 1: 1}

    o_f, o_b, s_fin = pl.pallas_call(
        _hgrn_kernel,
        grid=(BATCH // G, n_chunks),
        in_specs=in_specs,
        out_specs=[spec(0, False), spec(0, True), state_spec],
        out_shape=[o_shape, o_shape, jax.ShapeDtypeStruct((BATCH, 2, HG_HEADS, HG_D, HG_D), F32)],
        scratch_shapes=[pltpu.VMEM((G, 2, HG_HEADS, HG_D, HG_D), F32)],
        input_output_aliases=aliases,
        compiler_params=_params("parallel", "arbitrary"),
        name="hgrn2_%s" % ("latent" if latent else "context"),
    )(*args)
    return o_f.reshape(T_ALL, HG_WIDTH), o_b.reshape(T_ALL, HG_WIDTH), s_fin


def _hgrn_mixer(p_all):
    zero_s = jnp.zeros((BATCH, 2, HG_HEADS, HG_D, HG_D), F32)
    o_f, o_b, s_ctx = _hgrn_call(p_all, zero_s, latent=False)
    o_f, o_b, _ = _hgrn_call(p_all, s_ctx, latent=True, out_bufs=(o_f, o_b))
    return o_f, o_b


def _rg_gates(xc, wa_ref, ba_ref, wx_ref, bx_ref, lam_ref):
    xcb = xc.astype(BF16)
    r_parts, i_parts = [], []
    for n in range(RG_BLOCKS):
        sl = slice(n * RG_BW, (n + 1) * RG_BW)
        r_parts.append(jnp.dot(xcb[:, sl], wa_ref[n], preferred_element_type=F32))
        i_parts.append(jnp.dot(xcb[:, sl], wx_ref[n], preferred_element_type=F32))
    t_r = jnp.tanh(jnp.concatenate(r_parts, axis=1) + 0.5 * ba_ref[...])
    t_i = jnp.tanh(jnp.concatenate(i_parts, axis=1) + 0.5 * bx_ref[...])
    nl = -lam_ref[...]
    softplus = jnp.maximum(nl, 0.0) + jnp.log1p(jnp.exp(-jnp.abs(nl)))
    half_c = (0.5 * RG_C) * softplus
    neg_log_a = half_c + half_c * t_r
    a = jnp.exp(-neg_log_a)
    half_x = 0.5 * xc
    return a, jnp.sqrt(jnp.tanh(neg_log_a) * (1.0 + a * a)) * (half_x + half_x * t_i)


def _rg_ctx_kernel(*refs, rev, final):
    x_ref, cw_ref, cb_ref, wa_ref, ba_ref, wx_ref, bx_ref, lam_ref = refs[:8]
    if final:
        hb_ref, gate_ref, o_ref, hfin_ref = refs[8:]
    else:
        o_ref, hfin_ref = refs[8:]
    R = CTX_LEN
    N = R + 2 * SUBLANES
    x = x_ref[...]
    pad = jnp.zeros((SUBLANES, RG_WIDTH), F32)
    ext = jnp.concatenate([pad, x, pad], axis=0)
    lo, hi_ = SUBLANES, SUBLANES + R
    xc = (cw_ref[0:1, :] * pltpu.roll(ext, 2, 0)[lo:hi_]
          + cw_ref[1:2, :] * pltpu.roll(ext, 1, 0)[lo:hi_]
          + cw_ref[2:3, :] * x
          + cw_ref[3:4, :] * pltpu.roll(ext, N - 1, 0)[lo:hi_]
          + cb_ref[...])
    a, bt = _rg_gates(xc, wa_ref, ba_ref, wx_ref, bx_ref, lam_ref)

    row = lax.broadcasted_iota(jnp.int32, (R, RG_WIDTH), 0)
    step = 1
    while step < R:
        if rev:
            keep = row < R - step
            shift = R - step
        else:
            keep = row >= step
            shift = step
        a_sh = jnp.where(keep, pltpu.roll(a, shift, 0), 1.0)
        b_sh = jnp.where(keep, pltpu.roll(bt, shift, 0), 0.0)
        bt = bt + a * b_sh
        a = a * a_sh
        step *= 2
    h = bt
    end = 0 if rev else R - 1
    hfin_ref[...] = h[end:end + 1, :]
    if final:
        o_ref[...] = ((h + hb_ref[...]) * gate_ref[...]).astype(o_ref.dtype)
    else:
        o_ref[...] = h


def _rg_lat_kernel(*refs, rev, final):
    xp_ref, x_ref, xn_ref, cw_ref, cb_ref, wa_ref, ba_ref, wx_ref, bx_ref, lam_ref, h0_ref = refs[:11]
    if final:
        hb_ref, gate_ref, _alias, o_ref, ext, a_scr, h_scr, carry = refs[11:]
    else:
        _alias, o_ref, ext, a_scr, h_scr, carry = refs[11:]
    S = SLAB
    j = pl.program_id(1)
    band = (BANDS - 1 - j) if rev else j

    @pl.when(j == 0)
    def _():
        carry[...] = h0_ref[...]

    row = lax.broadcasted_iota(jnp.int32, (S, RG_WIDTH), 0)
    has_prev = (band > 0).astype(F32)
    has_next = (band < BANDS - 1).astype(F32)

    def shift_down(slab, top):
        return jnp.where(row == 0, top, pltpu.roll(slab, 1, 0))

    def shift_up(slab, bottom):
        return jnp.where(row == S - 1, bottom, pltpu.roll(slab, S - 1, 0))

    ext[0:S, :] = shift_down(x_ref[BAND - 2 * S:BAND - S, :], xp_ref[S - 1:S, :] * has_prev)
    ext[S:2 * S, :] = shift_down(x_ref[BAND - S:BAND, :], xp_ref[2 * S - 1:2 * S, :] * has_prev)
    ext[2 * S:2 * S + BAND, :] = x_ref[...]
    ext[2 * S + BAND:3 * S + BAND, :] = shift_up(x_ref[0:S, :], xn_ref[0:1, :] * has_next)

    CH = 256
    for s in range(0, BAND, CH):
        xc = (cw_ref[0:1, :] * ext[s:s + CH, :]
              + cw_ref[1:2, :] * ext[s + S:s + S + CH, :]
              + cw_ref[2:3, :] * ext[s + 2 * S:s + 2 * S + CH, :]
              + cw_ref[3:4, :] * ext[s + 3 * S:s + 3 * S + CH, :]
              + cb_ref[...])
        a, bt = _rg_gates(xc, wa_ref, ba_ref, wx_ref, bx_ref, lam_ref)
        a_scr[s:s + CH, :] = a
        h_scr[s:s + CH, :] = bt

    def body(t, hc):
        h, ac = hc
        c = (GRID_W - 1 - t) if rev else t
        sl = pl.ds(pl.multiple_of(c * S, S), S)
        a_c = a_scr[sl, :]
        h = a_c * h + h_scr[sl, :]
        ac = a_c * ac
        h_scr[sl, :] = h
        a_scr[sl, :] = ac
        return h, ac

    h_end, a_end = lax.fori_loop(0, GRID_W, body,
                                 (jnp.zeros((S, RG_WIDTH), F32), jnp.ones((S, RG_WIDTH), F32)),
                                 unroll=4)
    cin = jnp.zeros((S, RG_WIDTH), F32)
    c_run = carry[...]
    order = range(S - 1, -1, -1) if rev else range(S)
    for r in order:
        cin = jnp.where(row == r, c_run, cin)
        c_run = h_end[r:r + 1, :] + a_end[r:r + 1, :] * c_run
    carry[...] = c_run
    h = (h_scr[...].reshape(GRID_W, S, RG_WIDTH)
         + a_scr[...].reshape(GRID_W, S, RG_WIDTH) * cin[None]).reshape(BAND, RG_WIDTH)
    if final:
        o_ref[...] = ((h + hb_ref[...]) * gate_ref[...]).astype(o_ref.dtype)
    else:
        o_ref[...] = h


def _rg_call(p_all, w, *, rev, h_back=None):
    cw, cb, wa, ba, wx, bx, lam = w
    final = h_back is not None
    out_dtype = BF16 if final else F32

    def const(shape):
        return pl.BlockSpec(shape, lambda *_: (0,) * len(shape))

    w_specs = [const((4, RG_WIDTH)), const((1, RG_WIDTH)),
               const((RG_BLOCKS, RG_BW, RG_BW)), const((1, RG_WIDTH)),
               const((RG_BLOCKS, RG_BW, RG_BW)), const((1, RG_WIDTH)), const((1, RG_WIDTH))]
    w_args = [cw, cb, wa, ba, wx, bx, lam]
    carry_spec = pl.BlockSpec((None, 1, RG_WIDTH), lambda b, *_: (b, 0, 0))
    carry_shape = jax.ShapeDtypeStruct((BATCH, 1, RG_WIDTH), F32)

    def cmain(col):
        return pl.BlockSpec((CTX_LEN, RG_WIDTH), lambda b: (T_LAT // CTX_LEN + b, col))

    in_specs = [cmain(COL_RX)] + w_specs
    args = [p_all] + w_args
    if final:
        in_specs += [cmain(0), cmain(COL_RGATE)]
        args += [h_back, p_all]
    out_c, h_ctx = pl.pallas_call(
        functools.partial(_rg_ctx_kernel, rev=rev, final=final),
        grid=(BATCH,),
        in_specs=in_specs,
        out_specs=[cmain(0), carry_spec],
        out_shape=[jax.ShapeDtypeStruct((T_ALL, RG_WIDTH), out_dtype), carry_shape],
        compiler_params=_params("parallel"),
        name="rglru_context_%s" % ("bwd" if rev else "fwd"),
    )(*args)

    def band_idx(b, j):
        return b * BANDS + ((BANDS - 1 - j) if rev else j)

    def lmain(col):
        return pl.BlockSpec((BAND, RG_WIDTH), lambda b, j: (band_idx(b, j), col))

    per = BAND // (2 * SLAB)
    prev_spec = pl.BlockSpec((2 * SLAB, RG_WIDTH),
                             lambda b, j: (jnp.maximum(band_idx(b, j) * per - 1, 0), COL_RX))
    next_spec = pl.BlockSpec((SLAB, RG_WIDTH),
                             lambda b, j: ((band_idx(b, j) + 1) * (BAND // SLAB), COL_RX))
    in_specs = [prev_spec, lmain(COL_RX), next_spec] + w_specs + [carry_spec]
    args = [p_all, p_all, p_all] + w_args + [h_ctx]
    if final:
        in_specs += [lmain(0), lmain(COL_RGATE)]
        args += [h_back, p_all]
    in_specs.append(pl.BlockSpec(memory_space=pl.ANY))
    args.append(out_c)
    return pl.pallas_call(
        functools.partial(_rg_lat_kernel, rev=rev, final=final),
        grid=(BATCH, BANDS),
        in_specs=in_specs,
        out_specs=lmain(0),
        out_shape=jax.ShapeDtypeStruct((T_ALL, RG_WIDTH), out_dtype),
        scratch_shapes=[pltpu.VMEM((BAND + 3 * SLAB, RG_WIDTH), F32),
                        pltpu.VMEM((BAND, RG_WIDTH), F32),
                        pltpu.VMEM((BAND, RG_WIDTH), F32),
                        pltpu.VMEM((1, RG_WIDTH), F32)],
        input_output_aliases={len(args) - 1: 0},
        compiler_params=_params("parallel", "arbitrary"),
        name="rglru_latent_%s" % ("bwd" if rev else "fwd"),
    )(*args)


def _rg_mixer(p_all, cw, cb, wa, ba, wx, bx, lam):
    row = lambda t: t.reshape(1, RG_WIDTH)

    def weights(d):
        return (cw, row(cb), (0.5 * wa[d]).astype(BF16), row(ba[d]), (0.5 * wx[d]).astype(BF16),
                row(bx[d]), row(lam[d]))

    h_b = _rg_call(p_all, weights(1), rev=True)
    return _rg_call(p_all, weights(0), rev=False, h_back=h_b)


def _outproj_kernel(of_ref, ob_ref, sg_ref, hgg_ref, rg_ref, w_ref, x_ref, mod_ref, g_ref, o_ref,
                    h_ref):
    sub = o_ref.shape[0] // OUTPROJ_SUBTILES
    for t in range(OUTPROJ_SUBTILES):
        rs = slice(t * sub, (t + 1) * sub)
        o = of_ref[rs, :] + ob_ref[rs, :]
        gain = hgg_ref[...] * sg_ref[rs, :]
        parts = []
        for h in range(HG_HEADS):
            sl = slice(h * HG_D, (h + 1) * HG_D)
            o_h = o[:, sl]
            parts.append((o_h * _rms_scale(o_h) * gain[:, sl]).astype(BF16))
        hg = jnp.concatenate(parts, axis=1)
        mix = (jnp.dot(hg, w_ref[0:HG_WIDTH, :], preferred_element_type=F32)
               + jnp.dot(rg_ref[rs, :], w_ref[HG_WIDTH:, :], preferred_element_type=F32))
        y = mix * _rms_scale(mix) * g_ref[1:2, :]
        x_new = x_ref[rs, :] + mod_ref[2:3, :] * y
        o_ref[rs, :] = x_new
        hn = x_new * _rms_scale(x_new) * g_ref[2:3, :]
        h_ref[rs, :] = (hn * (1.0 + mod_ref[4:5, :]) + mod_ref[3:4, :]).astype(BF16)


def _outproj(o_f, o_b, p_all, hgg, rg, w_out, l, x_all, mod_l, norm_g_l, n_rows):
    tm = 512
    row = lambda col: pl.BlockSpec((tm, HG_WIDTH), lambda i: (i, col))
    return pl.pallas_call(
        _outproj_kernel,
        grid=(n_rows // tm,),
        in_specs=[
            row(0), row(0), row(COL_G),
            pl.BlockSpec((1, HG_WIDTH), lambda i: (0, 0)),
            row(0),
            pl.BlockSpec((None, D_MODEL, D_MODEL), lambda i: (l, 0, 0), pipeline_mode=pl.Buffered(1)),
            pl.BlockSpec((tm, D_MODEL), lambda i: (i, 0)),
            pl.BlockSpec((None, 6, D_MODEL), lambda i: (_mod_row(i, tm), 0, 0)),
            pl.BlockSpec((4, D_MODEL), lambda i: (0, 0)),
        ],
        out_specs=[pl.BlockSpec((tm, D_MODEL), lambda i: (i, 0)),
                   pl.BlockSpec((tm, D_MODEL), lambda i: (i, 0))],
        out_shape=[jax.ShapeDtypeStruct((T_ALL, D_MODEL), F32),
                   jax.ShapeDtypeStruct((T_ALL, D_MODEL), BF16)],
        input_output_aliases={6: 0},
        compiler_params=_params("parallel"),
        name="out_projection",
    )(o_f, o_b, p_all, hgg, rg, w_out, x_all, mod_l, norm_g_l)


FFN_TM = BAND
FFN_HALO = SLAB


def _ffn_up_kernel(*refs, banded, tile0):
    hp_ref, hm_ref, hn_ref, wa_ref, wv_ref, cwa_ref, cwv_ref, cba_ref, cbv_ref = refs[:9]
    o_ref, h_ref = refs[-2:]
    i = pl.program_id(0) + tile0
    n = pl.program_id(1)
    tm, H, S = FFN_TM, FFN_HALO, SLAB
    N = tm + 2 * H

    @pl.when(n == 0)
    def _():
        hp, hn = hp_ref[...], hn_ref[...]
        if banded:
            band = i % BANDS
            hp = jnp.where(band > 0, hp, jnp.zeros_like(hp))
            hn = jnp.where(band < BANDS - 1, hn, jnp.zeros_like(hn))
        h_ref[0:H, :] = hp
        h_ref[H:H + tm, :] = hm_ref[...]
        h_ref[H + tm:N, :] = hn

    if banded:
        row = lax.broadcasted_iota(jnp.int32, (S, o_ref.shape[1]), 0)

        def conv(u, cw, cb):
            mid = u[H:H + tm]
            first = jnp.where(row == 0, u[H - 1:H], pltpu.roll(mid[tm - S:tm], 1, 0))
            last = jnp.where(row == S - 1, u[H + tm:H + tm + 1], pltpu.roll(mid[0:S], S - 1, 0))
            prev = jnp.concatenate([first, mid[0:tm - S]], axis=0)
            nxt = jnp.concatenate([mid[S:tm], last], axis=0)
            return cw[0:1] * prev + cw[1:2] * mid + cw[2:3] * nxt + cb
    else:
        pos = lax.broadcasted_iota(jnp.int32, (tm, 1), 0) % CTX_LEN
        has_prev = (pos != 0).astype(F32)
        has_next = (pos != CTX_LEN - 1).astype(F32)

        def conv(u, cw, cb):
            return (cw[0:1] * (pltpu.roll(u, 1, 0)[H:H + tm] * has_prev)
                    + cw[1:2] * u[H:H + tm]
                    + cw[2:3] * (pltpu.roll(u, N - 1, 0)[H:H + tm] * has_next)
                    + cb)

    hh = h_ref[...]
    ua = conv(jnp.dot(hh, wa_ref[...].astype(BF16), preferred_element_type=F32), cwa_ref[...], cba_ref[...])
    uv = conv(jnp.dot(hh, wv_ref[...].astype(BF16), preferred_element_type=F32), cwv_ref[...], cbv_ref[...])
    o_ref[...] = (_gelu_tanh(ua) * uv).astype(o_ref.dtype)


def _ffn_up_call(h_all, w_up, l, cw_l, cb2, *, banded, tile0, n_tiles, n_rows, out_buf):
    tm, tn = FFN_TM, 512
    nn = D_FF // tn
    per = tm // FFN_HALO
    last = n_rows // FFN_HALO - 1
    in_specs = [
        pl.BlockSpec((FFN_HALO, D_MODEL), lambda i, n: (jnp.maximum((i + tile0) * per - 1, 0), 0)),
        pl.BlockSpec((tm, D_MODEL), lambda i, n: (i + tile0, 0)),
        pl.BlockSpec((FFN_HALO, D_MODEL), lambda i, n: (jnp.minimum((i + tile0 + 1) * per, last), 0)),
        pl.BlockSpec((None, D_MODEL, tn), lambda i, n: (l, 0, n)),
        pl.BlockSpec((None, D_MODEL, tn), lambda i, n: (l, 0, nn + n)),
        pl.BlockSpec((3, tn), lambda i, n: (0, n)),
        pl.BlockSpec((3, tn), lambda i, n: (0, nn + n)),
        pl.BlockSpec((1, tn), lambda i, n: (0, n)),
        pl.BlockSpec((1, tn), lambda i, n: (0, nn + n)),
    ]
    args = [h_all, h_all, h_all, w_up, w_up, cw_l, cw_l, cb2, cb2]
    aliases = {}
    if out_buf is not None:
        in_specs.append(pl.BlockSpec(memory_space=pl.ANY))
        args.append(out_buf)
        aliases = {len(args) - 1: 0}
    return pl.pallas_call(
        functools.partial(_ffn_up_kernel, banded=banded, tile0=tile0),
        grid=(n_tiles, nn),
        in_specs=in_specs,
        out_specs=pl.BlockSpec((tm, tn), lambda i, n: (i + tile0, n)),
        out_shape=jax.ShapeDtypeStruct((n_rows, D_FF), BF16),
        scratch_shapes=[pltpu.VMEM((tm + 2 * FFN_HALO, D_MODEL), BF16)],
        input_output_aliases=aliases,
        compiler_params=_params("parallel", "arbitrary"),
        name="ffn_up_%s" % ("latent" if banded else "context"),
    )(*args)


def _ffn_up(h_all, w_up, l, cw_l, cb_l, n_rows):
    cb2 = cb_l.reshape(1, 2 * D_FF)
    act = _ffn_up_call(h_all, w_up, l, cw_l, cb2, banded=True, tile0=0,
                       n_tiles=T_LAT // FFN_TM, n_rows=n_rows, out_buf=None)
    if n_rows > T_LAT:
        act = _ffn_up_call(h_all, w_up, l, cw_l, cb2, banded=False,
                           tile0=T_LAT // FFN_TM, n_tiles=T_CTX // FFN_TM, n_rows=n_rows, out_buf=act)
    return act


def _ffn_down_kernel(a_ref, w_ref, x_ref, mod_ref, g_ref, o_ref):
    f = jnp.dot(a_ref[...], w_ref[...], preferred_element_type=F32)
    y = f * _rms_scale(f) * g_ref[3:4, :]
    o_ref[...] = x_ref[...] + mod_ref[5:6, :] * y


def _ffn_down(act, w_down, l, x_all, mod_l, norm_g_l, n_rows, in_place):
    tm = 512
    return pl.pallas_call(
        _ffn_down_kernel,
        grid=(n_rows // tm,),
        in_specs=[
            pl.BlockSpec((tm, D_FF), lambda i: (i, 0)),
            pl.BlockSpec((None, D_FF, D_MODEL), lambda i: (l, 0, 0), pipeline_mode=pl.Buffered(1)),
            pl.BlockSpec((tm, D_MODEL), lambda i: (i, 0)),
            pl.BlockSpec((None, 6, D_MODEL), lambda i: (_mod_row(i, tm), 0, 0)),
            pl.BlockSpec((4, D_MODEL), lambda i: (0, 0)),
        ],
        out_specs=pl.BlockSpec((tm, D_MODEL), lambda i: (i, 0)),
        out_shape=jax.ShapeDtypeStruct((T_ALL if in_place else n_rows, D_MODEL), F32),
        input_output_aliases={2: 0} if in_place else {},
        compiler_params=_params("parallel"),
        name="ffn_down",
    )(act, w_down, x_all, mod_l, norm_g_l)


def kernel(x, c, ctx, c_ctx, w_mod, b_mod, norm_g, w_in, hg_lower_bounds, hg_norm_g, rg_conv_w,
           rg_conv_b, rg_wa, rg_ba, rg_wx, rg_bx, rg_lambda, w_out, ffn_w_up, ffn_conv_w,
           ffn_conv_b, ffn_w_down):
    x_all = jnp.concatenate([_to_banded(x), ctx.reshape(T_CTX, D_MODEL)], axis=0)
    cc = jnp.concatenate([c, c_ctx[None, :], jnp.zeros((MOD_ROWS - BATCH - 1, D_MODEL), F32)], axis=0)
    mods = _modulation(cc, w_mod, b_mod)
    lb_all = _lower_bounds(hg_lower_bounds)

    w_in, w_out, ffn_w_down = (w.astype(BF16) for w in (w_in, w_out, ffn_w_down))

    for l in range(DEPTH):
        ctx_out = l < DEPTH - 1
        n_rows = T_ALL if ctx_out else T_LAT
        p_all = _inproj(x_all, mods[l], norm_g[l], lb_all[l], w_in, l)
        o_f, o_b = _hgrn_mixer(p_all)
        rg = _rg_mixer(p_all, rg_conv_w[l], rg_conv_b[l], rg_wa[l], rg_ba[l], rg_wx[l], rg_bx[l],
                       rg_lambda[l])
        x_all, h_all = _outproj(o_f, o_b, p_all, hg_norm_g[l].reshape(1, HG_WIDTH), rg, w_out, l, x_all,
                                mods[l], norm_g[l], n_rows)
        act = _ffn_up(h_all, ffn_w_up, l, ffn_conv_w[l], ffn_conv_b[l], n_rows)
        x_all = _ffn_down(act, ffn_w_down, l, x_all, mods[l], norm_g[l], n_rows, in_place=ctx_out)
    return _from_banded(x_all)
```
